```python
import jax, jax.numpy as jnp
from jax import lax
import numpy as np

D_MODEL = 2048
BATCH = 4
SEQ = 4096
DEPTH = 2

D_MIX = D_MODEL
GLA_HEADS = 4
GLA_DV = D_MIX // 2 // GLA_HEADS
GLA_DK = GLA_DV // 2
GLA_GATE_RANK = 16
GLA_TAU = 16.0
GLA_CHUNK = 64
RET_HEADS = 4
RET_DV = D_MIX // 2 // RET_HEADS
RET_DK = RET_DV
RET_CHUNK = 64
ROPE_BASE = 10000.0
D_FF = 4 * D_MODEL
EPS = 1e-6
MAX_POS_OFFSET = 1024

GLA_QK = GLA_HEADS * GLA_DK
GLA_V = GLA_HEADS * GLA_DV
RET_QK = RET_HEADS * RET_DK
RET_V = RET_HEADS * RET_DV
D_IN_PROJ = 2 * GLA_QK + 2 * GLA_V + GLA_GATE_RANK + 2 * RET_QK + 2 * RET_V

kernel_name = "hybrid_gla_retention_sqrelu"


def _split_points():
    widths = [GLA_QK, GLA_QK, GLA_V, GLA_V, GLA_GATE_RANK, RET_QK, RET_QK, RET_V, RET_V]
    pts, acc = [], 0
    for w in widths[:-1]:
        acc += w
        pts.append(acc)
    return pts


def rms_norm(x, g):
    xf = x.astype(jnp.float32)
    y = xf * lax.rsqrt(jnp.mean(xf * xf, axis=-1, keepdims=True) + EPS)
    return (y * g.astype(jnp.float32)).astype(x.dtype)


def head_rms_norm(o, n_heads, g):
    b, s, _ = o.shape
    of = o.astype(jnp.float32).reshape(b, s, n_heads, -1)
    y = of * lax.rsqrt(jnp.mean(of * of, axis=-1, keepdims=True) + EPS)
    return y.reshape(b, s, -1) * g.astype(jnp.float32)


def head_group_norm(o, n_heads, g, beta):
    b, s, _ = o.shape
    of = o.astype(jnp.float32).reshape(b, s, n_heads, -1)
    mu = jnp.mean(of, axis=-1, keepdims=True)
    c = of - mu
    y = c * lax.rsqrt(jnp.mean(c * c, axis=-1, keepdims=True) + EPS)
    return y.reshape(b, s, -1) * g.astype(jnp.float32) + beta.astype(jnp.float32)


def to_chunks(t, n_heads, chunk):
    b, s, _ = t.shape
    return t.reshape(b, s // chunk, chunk, n_heads, -1).transpose(0, 3, 1, 2, 4)


def from_chunks(t):
    b, h, n, c, d = t.shape
    return t.transpose(0, 2, 3, 1, 4).reshape(b, n * c, h * d)


def chunk_state_scan(decay, local):
    def step(s, inp):
        d, u = inp
        return d[..., None] * s + u, s
    s0 = jnp.zeros_like(local[:, :, 0])
    _, states = lax.scan(step, s0, (jnp.moveaxis(decay, 2, 0), jnp.moveaxis(local, 2, 0)))
    return jnp.moveaxis(states, 0, 2)


def gla_mixer(q, k, v, log_a):
    c = GLA_CHUNK
    q = to_chunks(q, GLA_HEADS, c) * (GLA_DK ** -0.5)
    k = to_chunks(k, GLA_HEADS, c)
    v = to_chunks(v, GLA_HEADS, c)
    g = to_chunks(log_a.astype(jnp.float32), GLA_HEADS, c)
    b = jnp.cumsum(g, axis=3)
    b_last = b[:, :, :, -1:, :]
    q_dec = q * jnp.exp(b)
    k_dec = k * jnp.exp(-b)
    k_tail = k * jnp.exp(b_last - b)
    causal = jnp.tril(jnp.ones((c, c), dtype=bool))
    scores = jnp.where(causal, jnp.einsum('bhnik,bhnjk->bhnij', q_dec, k_dec), 0.0)
    o_intra = jnp.einsum('bhnij,bhnjv->bhniv', scores, v)
    local = jnp.einsum('bhnjk,bhnjv->bhnkv', k_tail, v)
    states = chunk_state_scan(jnp.exp(b_last[:, :, :, 0, :]), local)
    o_inter = jnp.einsum('bhnik,bhnkv->bhniv', q_dec, states)
    return from_chunks(o_intra + o_inter)


def apply_rotary(t, cos, sin):
    half = t.shape[-1] // 2
    t1, t2 = t[..., :half], t[..., half:]
    return jnp.concatenate([t1 * cos - t2 * sin, t2 * cos + t1 * sin], axis=-1)


def retention_mixer(q, k, v, cos, sin):
    c = RET_CHUNK
    bsz, s, _ = q.shape
    q = apply_rotary(q.reshape(bsz, s, RET_HEADS, RET_DK), cos, sin).reshape(bsz, s, -1)
    k = apply_rotary(k.reshape(bsz, s, RET_HEADS, RET_DK), cos, sin).reshape(bsz, s, -1)
    q = to_chunks(q, RET_HEADS, c)
    k = to_chunks(k, RET_HEADS, c) * (RET_DK ** -0.5)
    v = to_chunks(v, RET_HEADS, c)
    n_chunks = q.shape[2]
    heads = jnp.arange(RET_HEADS, dtype=jnp.float32)
    log_gamma = jnp.log1p(-(2.0 ** (-5.0 - heads)))
    idx = jnp.arange(c, dtype=jnp.float32)
    rel = idx[:, None] - idx[None, :]
    decay_mask = jnp.where(rel >= 0.0,
                           jnp.exp(log_gamma[:, None, None] * jnp.maximum(rel, 0.0)), 0.0)
    scores = jnp.einsum('bhnid,bhnjd->bhnij', q, k) * decay_mask[None, :, None]
    o_intra = jnp.einsum('bhnij,bhnjv->bhniv', scores, v)
    inner = jnp.exp(log_gamma[:, None] * (idx[None, :] + 1.0))
    tail = jnp.exp(log_gamma[:, None] * (c - 1.0 - idx[None, :]))
    local = jnp.einsum('bhnjd,bhnjv->bhndv', k * tail[None, :, None, :, None], v)
    chunk_decay = jnp.broadcast_to(jnp.exp(log_gamma * c)[None, :, None, None],
                                   (bsz, RET_HEADS, n_chunks, RET_DK))
    states = chunk_state_scan(chunk_decay, local)
    o_inter = jnp.einsum('bhnid,bhndv->bhniv', q, states) * inner[None, :, None, :, None]
    return from_chunks(o_intra + o_inter)


def setup_inputs(seed: int = 0) -> dict:
    key = jax.random.key(seed)
    ks = jax.random.split(key, 16)
    f32 = jnp.float32
    x = jax.random.normal(ks[0], (BATCH, SEQ, D_MODEL), f32)
    offsets = jax.random.randint(ks[1], (BATCH, 1), 0, MAX_POS_OFFSET, dtype=jnp.int32)
    positions = offsets + jnp.arange(SEQ, dtype=jnp.int32)[None, :]
    def gain(k, n):
        return 1.0 + 0.02 * jax.random.normal(k, (DEPTH, n), f32)
    return {
        "x": x,
        "positions": positions,
        "attn_norm": gain(ks[2], D_MODEL),
        "w_in": jax.random.normal(ks[3], (DEPTH, D_MODEL, D_IN_PROJ), f32) * D_MODEL ** -0.5,
        "gla_gate_up": jax.random.normal(ks[4], (DEPTH, GLA_GATE_RANK, GLA_QK), f32) * GLA_GATE_RANK ** -0.5,
        "gla_gate_bias": 0.1 * jax.random.normal(ks[5], (DEPTH, GLA_QK), f32),
        "gla_out_norm": gain(ks[6], GLA_V),
        "ret_norm_gain": gain(ks[7], RET_V),
        "ret_norm_bias": 0.02 * jax.random.normal(ks[8], (DEPTH, RET_V), f32),
        "w_out": jax.random.normal(ks[9], (DEPTH, D_MIX, D_MODEL), f32) * D_MIX ** -0.5,
        "mlp_norm": gain(ks[10], D_MODEL),
        "w_up": jax.random.normal(ks[11], (DEPTH, D_MODEL, D_FF), f32) * D_MODEL ** -0.5,
        "w_down": jax.random.normal(ks[12], (DEPTH, D_FF, D_MODEL), f32) * D_FF ** -0.5,
        "final_norm": 1.0 + 0.02 * jax.random.normal(ks[13], (D_MODEL,), f32),
    }


def reference(x, positions, attn_norm, w_in, gla_gate_up, gla_gate_bias, gla_out_norm,
              ret_norm_gain, ret_norm_bias, w_out, mlp_norm, w_up, w_down, final_norm):
    inv_freq = ROPE_BASE ** (-jnp.arange(0, RET_DK, 2, dtype=jnp.float32) / RET_DK)
    ang = positions.astype(jnp.float32)[..., None] * inv_freq
    cos = jnp.cos(ang)[:, :, None, :]
    sin = jnp.sin(ang)[:, :, None, :]
    split_pts = _split_points()
    for l in range(DEPTH):
        h = rms_norm(x, attn_norm[l])
        proj = h @ w_in[l]
        gq, gk, gv, gg, gr, rq, rk, rv, rg = jnp.split(proj, split_pts, axis=-1)
        log_a = jax.nn.log_sigmoid((gr @ gla_gate_up[l] + gla_gate_bias[l]).astype(jnp.float32)) / GLA_TAU
        o_gla = gla_mixer(gq, gk, gv, log_a)
        o_gla = head_rms_norm(o_gla, GLA_HEADS, gla_out_norm[l]) * jax.nn.silu(gg.astype(jnp.float32))
        o_ret = retention_mixer(rq, rk, rv, cos, sin)
        o_ret = head_group_norm(o_ret, RET_HEADS, ret_norm_gain[l], ret_norm_bias[l]) * jax.nn.silu(rg.astype(jnp.float32))
        mix = jnp.concatenate([o_gla, o_ret], axis=-1).astype(x.dtype)
        x = x + mix @ w_out[l]
        h = rms_norm(x, mlp_norm[l])
        x = x + jnp.square(jax.nn.relu(h @ w_up[l])) @ w_down[l]
    return rms_norm(x, final_norm)
```

```python
import functools
import math

import jax
import jax.numpy as jnp
from jax import lax
from jax.experimental import pallas as pl
from jax.experimental.pallas import tpu as pltpu

D_MODEL = 2048
GLA_HEADS = 4
GLA_DK = 128
GLA_DV = 256
GLA_GATE_RANK = 16
GLA_TAU = 16.0
GLA_CHUNK = 64
RET_HEADS = 4
RET_DK = 256
RET_DV = 256
ROPE_BASE = 10000.0
D_FF = 4 * D_MODEL
EPS = 1e-6

GLA_QK = GLA_HEADS * GLA_DK
GLA_V = GLA_HEADS * GLA_DV
RET_QK = RET_HEADS * RET_DK
RET_V = RET_HEADS * RET_DV
D_PROJ = 2 * GLA_QK + 2 * GLA_V + 2 * RET_QK + 2 * RET_V

OFF_GQ = 0
OFF_GK = OFF_GQ + GLA_QK
OFF_GV = OFF_GK + GLA_QK
OFF_GG = OFF_GV + GLA_V
OFF_RQ = OFF_GG + GLA_V
OFF_RK = OFF_RQ + RET_QK
OFF_RV = OFF_RK + RET_QK
OFF_RG = OFF_RV + RET_V

LANES = 128
MIX_BLOCK = 256
VMEM_LIMIT = 56 * 1024 * 1024

F32 = jnp.float32
BF16 = jnp.bfloat16

_NT = (((1,), (1,)), ((), ()))
_TN = (((0,), (0,)), ((), ()))


def _params(n_axes):
    return pltpu.CompilerParams(
        dimension_semantics=("arbitrary",) * n_axes, vmem_limit_bytes=VMEM_LIMIT)


def _rms(x, gain):
    return x * lax.rsqrt(jnp.mean(x * x, axis=-1, keepdims=True) + EPS) * gain


def _rope_kernel(pos_ref, invf_ref, cos_ref, sin_ref):
    rows = pos_ref.shape[0]
    invf = invf_ref[...]
    for r in range(rows):
        p = pos_ref[r:r + 1, :].astype(F32)
        col = jnp.broadcast_to(p, (LANES, LANES)).T
        ang = col * invf
        cos_ref[r * LANES:(r + 1) * LANES, :] = jnp.cos(ang)
        sin_ref[r * LANES:(r + 1) * LANES, :] = jnp.sin(ang)


def _rope_tables(positions):
    n_tok = positions.size
    rows = n_tok // LANES
    rb = 8
    pos2d = positions.reshape(rows, LANES)
    inv_freq = (ROPE_BASE ** (-jnp.arange(0, RET_DK, 2, dtype=F32) / RET_DK)).reshape(1, RET_DK // 2)
    out = jax.ShapeDtypeStruct((n_tok, RET_DK // 2), F32)
    return pl.pallas_call(
        _rope_kernel,
        grid=(rows // rb,),
        in_specs=[pl.BlockSpec((rb, LANES), lambda i: (i, 0)),
                  pl.BlockSpec((1, RET_DK // 2), lambda i: (0, 0))],
        out_specs=[pl.BlockSpec((rb * LANES, RET_DK // 2), lambda i: (i, 0))] * 2,
        out_shape=[out, out],
        compiler_params=_params(1),
        name="rope_tables",
    )(pos2d, inv_freq)


def _inproj_kernel(x_ref, g_ref, w_ref, wgr_ref, gup_ref, gb_ref, proj_ref, loga_ref, h_ref):
    @pl.when(pl.program_id(1) == 0)
    def _():
        hb = _rms(x_ref[...], g_ref[...]).astype(BF16)
        h_ref[...] = hb
        gr = jnp.dot(hb, wgr_ref[...], preferred_element_type=F32)
        z = jnp.dot(gr.astype(BF16), gup_ref[...], preferred_element_type=F32) + gb_ref[...]
        log_sig = jnp.minimum(z, 0.0) - jnp.log1p(jnp.exp(-jnp.abs(z)))
        loga_ref[...] = log_sig / GLA_TAU

    proj_ref[...] = jnp.dot(h_ref[...], w_ref[...], preferred_element_type=F32)


def _inproj(x, gain, w_main, w_gr, gate_up, gate_bias, tm=1024, tn=1024):
    n_tok = x.shape[0]
    return pl.pallas_call(
        _inproj_kernel,
        grid=(n_tok // tm, D_PROJ // tn),
        in_specs=[pl.BlockSpec((tm, D_MODEL), lambda i, j: (i, 0)),
                  pl.BlockSpec((1, D_MODEL), lambda i, j: (0, 0)),
                  pl.BlockSpec((D_MODEL, tn), lambda i, j: (0, j)),
                  pl.BlockSpec((D_MODEL, LANES), lambda i, j: (0, 0)),
                  pl.BlockSpec((LANES, GLA_QK), lambda i, j: (0, 0)),
                  pl.BlockSpec((1, GLA_QK), lambda i, j: (0, 0))],
        out_specs=[pl.BlockSpec((tm, tn), lambda i, j: (i, j)),
                   pl.BlockSpec((tm, GLA_QK), lambda i, j: (i, 0))],
        out_shape=[jax.ShapeDtypeStruct((n_tok, D_PROJ), F32),
                   jax.ShapeDtypeStruct((n_tok, GLA_QK), F32)],
        scratch_shapes=[pltpu.VMEM((tm, D_MODEL), BF16)],
        compiler_params=_params(2),
        name="norm_inproj",
    )(x, gain, w_main, w_gr, gate_up, gate_bias)


def _split3(x):
    hi = x.astype(BF16)
    r1 = x - hi.astype(F32)
    mid = r1.astype(BF16)
    lo = (r1 - mid.astype(F32)).astype(BF16)
    return hi, mid, lo


def _mixer_kernel(proj_ref, loga_ref, cos_ref, sin_ref, gno_ref, rgain_ref, rbias_ref,
                  mix_ref, sg_ref, sr_ref):
    t_blk = MIX_BLOCK
    n_chunk = t_blk // GLA_CHUNK

    @pl.when(pl.program_id(1) == 0)
    def _():
        sg_ref[...] = jnp.zeros_like(sg_ref)
        sr_ref[...] = jnp.zeros_like(sr_ref)

    row = lax.broadcasted_iota(jnp.int32, (t_blk, t_blk), 0)
    col = lax.broadcasted_iota(jnp.int32, (t_blk, t_blk), 1)
    causal = col <= row
    chunk_shift = int(math.log2(GLA_CHUNK))
    gla_mask = causal & ((row >> chunk_shift) == (col >> chunk_shift))
    rel = jnp.maximum(row - col, 0).astype(F32)
    rowf = lax.broadcasted_iota(jnp.int32, (t_blk, RET_DV), 0).astype(F32)

    cum = gla_mask.astype(BF16)
    la_hi, la_mid, la_lo = _split3(loga_ref[...])
    b_all = (jnp.dot(cum, la_hi, preferred_element_type=F32)
             + jnp.dot(cum, la_mid, preferred_element_type=F32)
             + jnp.dot(cum, la_lo, preferred_element_type=F32))

    for h in range(GLA_HEADS):
        q = proj_ref[:, OFF_GQ + h * GLA_DK:OFF_GQ + (h + 1) * GLA_DK]
        k = proj_ref[:, OFF_GK + h * GLA_DK:OFF_GK + (h + 1) * GLA_DK]
        v = proj_ref[:, OFF_GV + h * GLA_DV:OFF_GV + (h + 1) * GLA_DV].astype(BF16)
        g = proj_ref[:, OFF_GG + h * GLA_DV:OFF_GG + (h + 1) * GLA_DV]
        b = b_all[:, h * GLA_DK:(h + 1) * GLA_DK]
        b_last = [b[(c + 1) * GLA_CHUNK - 1:(c + 1) * GLA_CHUNK, :] for c in range(n_chunk)]
        b_last_rows = jnp.concatenate(
            [jnp.broadcast_to(bl, (GLA_CHUNK, GLA_DK)) for bl in b_last], axis=0)
        q_dec = (q * (GLA_DK ** -0.5) * jnp.exp(b)).astype(BF16)
        k_dec = (k * jnp.exp(-b)).astype(BF16)
        k_tail = (k * jnp.exp(b_last_rows - b)).astype(BF16)

        s = lax.dot_general(q_dec, k_dec, _NT, preferred_element_type=F32)
        s = jnp.where(gla_mask, s, 0.0).astype(BF16)
        o_intra = jnp.dot(s, v, preferred_element_type=F32)

        state = sg_ref[h]
        o_parts = []
        for c in range(n_chunk):
            sl = slice(c * GLA_CHUNK, (c + 1) * GLA_CHUNK)
            o_parts.append(o_intra[sl] + jnp.dot(q_dec[sl], state.astype(BF16),
                                                 preferred_element_type=F32))
            local = lax.dot_general(k_tail[sl], v[sl], _TN, preferred_element_type=F32)
            dcol = jnp.broadcast_to(jnp.exp(b_last[c]), (GLA_DK, GLA_DK)).T
            state = jnp.concatenate([dcol, dcol], axis=1) * state + local
        sg_ref[h] = state

        o = jnp.concatenate(o_parts, axis=0)
        y = o * lax.rsqrt(jnp.mean(o * o, axis=-1, keepdims=True) + EPS)
        y = y * gno_ref[:, h * GLA_DV:(h + 1) * GLA_DV]
        y = y * (g / (1.0 + jnp.exp(-g)))
        mix_ref[:, h * GLA_DV:(h + 1) * GLA_DV] = y.astype(mix_ref.dtype)

    cos = cos_ref[...]
    sin = sin_ref[...]
    half = RET_DK // 2

    def rotary(t):
        t1, t2 = t[:, :half], t[:, half:]
        return jnp.concatenate([t1 * cos - t2 * sin, t2 * cos + t1 * sin], axis=1)

    for h in range(RET_HEADS):
        log_gamma = math.log1p(-(2.0 ** (-5.0 - h)))
        q = rotary(proj_ref[:, OFF_RQ + h * RET_DK:OFF_RQ + (h + 1) * RET_DK])
        k = rotary(proj_ref[:, OFF_RK + h * RET_DK:OFF_RK + (h + 1) * RET_DK]) * (RET_DK ** -0.5)
        v = proj_ref[:, OFF_RV + h * RET_DV:OFF_RV + (h + 1) * RET_DV].astype(BF16)
        g = proj_ref[:, OFF_RG + h * RET_DV:OFF_RG + (h + 1) * RET_DV]
        qb = q.astype(BF16)
        decay = jnp.where(causal, jnp.exp(log_gamma * rel), 0.0)
        s = lax.dot_general(qb, k.astype(BF16), _NT, preferred_element_type=F32) * decay
        o_intra = jnp.dot(s.astype(BF16), v, preferred_element_type=F32)
        k_tail = (k * jnp.exp(log_gamma * (t_blk - 1.0 - rowf))).astype(BF16)
        local = lax.dot_general(k_tail, v, _TN, preferred_element_type=F32)
        state = sr_ref[h]
        o_inter = jnp.dot(qb, state.astype(BF16), preferred_element_type=F32)
        o = o_intra + o_inter * jnp.exp(log_gamma * (rowf + 1.0))
        sr_ref[h] = math.exp(log_gamma * t_blk) * state + local

        mu = jnp.mean(o, axis=-1, keepdims=True)
        cen = o - mu
        y = cen * lax.rsqrt(jnp.mean(cen * cen, axis=-1, keepdims=True) + EPS)
        y = y * rgain_ref[:, h * RET_DV:(h + 1) * RET_DV] + rbias_ref[:, h * RET_DV:(h + 1) * RET_DV]
        y = y * (g / (1.0 + jnp.exp(-g)))
        mix_ref[:, GLA_V + h * RET_DV:GLA_V + (h + 1) * RET_DV] = y.astype(mix_ref.dtype)


def _mixers(proj, loga, cos, sin, gla_norm, ret_gain, ret_bias, batch, seq):
    n_tok = proj.shape[0]
    t_blk = MIX_BLOCK
    nt = seq // t_blk
    tok = lambda b, t: (b * nt + t, 0)
    const = lambda b, t: (0, 0)
    return pl.pallas_call(
        _mixer_kernel,
        grid=(batch, nt),
        in_specs=[pl.BlockSpec((t_blk, D_PROJ), tok),
                  pl.BlockSpec((t_blk, GLA_QK), tok),
                  pl.BlockSpec((t_blk, RET_DK // 2), tok),
                  pl.BlockSpec((t_blk, RET_DK // 2), tok),
                  pl.BlockSpec((1, GLA_V), const),
                  pl.BlockSpec((1, RET_V), const),
                  pl.BlockSpec((1, RET_V), const)],
        out_specs=pl.BlockSpec((t_blk, D_MODEL), tok),
        out_shape=jax.ShapeDtypeStruct((n_tok, D_MODEL), BF16),
        scratch_shapes=[pltpu.VMEM((GLA_HEADS, GLA_DK, GLA_DV), F32),
                        pltpu.VMEM((RET_HEADS, RET_DK, RET_DV), F32)],
        compiler_params=_params(2),
        name="mixers",
    )(proj, loga, cos, sin, gla_norm, ret_gain, ret_bias)


def _outproj_kernel(x_ref, mix_ref, w_ref, o_ref):
    o_ref[...] = x_ref[...] + jnp.dot(mix_ref[...], w_ref[...], preferred_element_type=F32)


def _outproj(x, mix, w_out, tm=1024, tn=1024):
    n_tok = x.shape[0]
    return pl.pallas_call(
        _outproj_kernel,
        grid=(n_tok // tm, D_MODEL // tn),
        in_specs=[pl.BlockSpec((tm, tn), lambda i, j: (i, j)),
                  pl.BlockSpec((tm, D_MODEL), lambda i, j: (i, 0)),
                  pl.BlockSpec((D_MODEL, tn), lambda i, j: (0, j))],
        out_specs=pl.BlockSpec((tm, tn), lambda i, j: (i, j)),
        out_shape=jax.ShapeDtypeStruct((n_tok, D_MODEL), F32),
        compiler_params=_params(2),
        name="outproj_residual",
    )(x, mix, w_out)


def _mlp_kernel(x_ref, g_ref, wu_ref, wd_ref, fg_ref, o_ref, h_ref, *, apply_final_norm):
    k = pl.program_id(1)

    @pl.when(k == 0)
    def _():
        x = x_ref[...]
        h_ref[...] = _rms(x, g_ref[...]).astype(BF16)
        o_ref[...] = x

    u = jnp.dot(h_ref[...], wu_ref[...], preferred_element_type=F32)
    a = jnp.square(jnp.maximum(u, 0.0)).astype(BF16)
    o_ref[...] += jnp.dot(a, wd_ref[...], preferred_element_type=F32)

    if apply_final_norm:
        @pl.when(k == pl.num_programs(1) - 1)
        def _():
            o_ref[...] = _rms(o_ref[...], fg_ref[...])


def _mlp(x, gain, w_up, w_down, final_gain, apply_final_norm, tm=1024, tf=512):
    n_tok = x.shape[0]
    return pl.pallas_call(
        functools.partial(_mlp_kernel, apply_final_norm=apply_final_norm),
        grid=(n_tok // tm, D_FF // tf),
        in_specs=[pl.BlockSpec((tm, D_MODEL), lambda i, k: (i, 0)),
                  pl.BlockSpec((1, D_MODEL), lambda i, k: (0, 0)),
                  pl.BlockSpec((D_MODEL, tf), lambda i, k: (0, k)),
                  pl.BlockSpec((tf, D_MODEL), lambda i, k: (k, 0)),
                  pl.BlockSpec((1, D_MODEL), lambda i, k: (0, 0))],
        out_specs=pl.BlockSpec((tm, D_MODEL), lambda i, k: (i, 0)),
        out_shape=jax.ShapeDtypeStruct((n_tok, D_MODEL), F32),
        scratch_shapes=[pltpu.VMEM((tm, D_MODEL), BF16)],
        compiler_params=_params(2),
        name="norm_mlp_residual",
    )(x, gain, w_up, w_down, final_gain)


def kernel(x, positions, attn_norm, w_in, gla_gate_up, gla_gate_bias, gla_out_norm,
           ret_norm_gain, ret_norm_bias, w_out, mlp_norm, w_up, w_down, final_norm):
    batch, seq, d_model = x.shape
    depth = w_in.shape[0]
    assert d_model == D_MODEL and seq % MIX_BLOCK == 0
    n_tok = batch * seq
    xf = x.reshape(n_tok, d_model)

    cos, sin = _rope_tables(positions)

    gr_lo = 2 * GLA_QK + 2 * GLA_V
    gr_hi = gr_lo + GLA_GATE_RANK
    for l in range(depth):
        w_main = jnp.concatenate([w_in[l, :, :gr_lo], w_in[l, :, gr_hi:]], axis=1).astype(BF16)
        w_gr = jnp.pad(w_in[l, :, gr_lo:gr_hi], ((0, 0), (0, LANES - GLA_GATE_RANK))).astype(BF16)
        gate_up = jnp.pad(gla_gate_up[l], ((0, LANES - GLA_GATE_RANK), (0, 0))).astype(BF16)
        proj, loga = _inproj(xf, attn_norm[l].reshape(1, -1), w_main, w_gr, gate_up,
                             gla_gate_bias[l].reshape(1, -1))
        mix = _mixers(proj, loga, cos, sin, gla_out_norm[l].reshape(1, -1),
                      ret_norm_gain[l].reshape(1, -1), ret_norm_bias[l].reshape(1, -1), batch, seq)
        xf = _outproj(xf, mix, w_out[l].astype(BF16))
        xf = _mlp(xf, mlp_norm[l].reshape(1, -1), w_up[l].astype(BF16), w_down[l].astype(BF16),
                  final_norm.reshape(1, -1), apply_final_norm=(l == depth - 1))
    return xf.reshape(batch, seq, d_model)
```

```python
import functools
import math

import jax
import jax.numpy as jnp
from jax import lax
from jax.experimental import pallas as pl
from jax.experimental.pallas import tpu as pltpu

D_MODEL = 2048
GLA_HEADS = 4
GLA_DK = 128
GLA_DV = 256
GLA_GATE_RANK = 16
GLA_TAU = 16.0
GLA_CHUNK = 64
RET_HEADS = 4
RET_DK = 256
RET_DV = 256
ROPE_BASE = 10000.0
D_FF = 4 * D_MODEL
EPS = 1e-6

GLA_QK = GLA_HEADS * GLA_DK
GLA_V = GLA_HEADS * GLA_DV
RET_QK = RET_HEADS * RET_DK
RET_V = RET_HEADS * RET_DV
D_PROJ = 2 * GLA_QK + 2 * GLA_V + 2 * RET_QK + 2 * RET_V

OFF_GQ = 0
OFF_GK = OFF_GQ + GLA_QK
OFF_GV = OFF_GK + GLA_QK
OFF_GG = OFF_GV + GLA_V
OFF_RQ = OFF_GG + GLA_V
OFF_RK = OFF_RQ + RET_QK
OFF_RV = OFF_RK + RET_QK
OFF_RG = OFF_RV + RET_V

LANES = 128
MIX_BLOCK = 256
VMEM_LIMIT = 56 * 1024 * 1024

F32 = jnp.float32
BF16 = jnp.bfloat16

_NT = (((1,), (1,)), ((), ()))
_TN = (((0,), (0,)), ((), ()))


def _params(n_axes):
    return pltpu.CompilerParams(
        dimension_semantics=("arbitrary",) * n_axes, vmem_limit_bytes=VMEM_LIMIT)


def _rms(x, gain):
    return x * lax.rsqrt(jnp.mean(x * x, axis=-1, keepdims=True) + EPS) * gain


def _rope_kernel(pos_ref, invf_ref, cos_ref, sin_ref):
    rows = pos_ref.shape[0]
    invf = invf_ref[...]
    for r in range(rows):
        p = pos_ref[r:r + 1, :].astype(F32)
        col = jnp.broadcast_to(p, (LANES, LANES)).T
        ang = col * invf
        cos_ref[r * LANES:(r + 1) * LANES, :] = jnp.cos(ang)
        sin_ref[r * LANES:(r + 1) * LANES, :] = jnp.sin(ang)


def _rope_tables(positions):
    n_tok = positions.size
    rows = n_tok // LANES
    rb = 8
    pos2d = positions.reshape(rows, LANES)
    inv_freq = (ROPE_BASE ** (-jnp.arange(0, RET_DK, 2, dtype=F32) / RET_DK)).reshape(1, RET_DK // 2)
    out = jax.ShapeDtypeStruct((n_tok, RET_DK // 2), F32)
    return pl.pallas_call(
        _rope_kernel,
        grid=(rows // rb,),
        in_specs=[pl.BlockSpec((rb, LANES), lambda i: (i, 0)),
                  pl.BlockSpec((1, RET_DK // 2), lambda i: (0, 0))],
        out_specs=[pl.BlockSpec((rb * LANES, RET_DK // 2), lambda i: (i, 0))] * 2,
        out_shape=[out, out],
        compiler_params=_params(1),
        name="rope_tables",
    )(pos2d, inv_freq)


def _inproj_kernel(x_ref, g_ref, w_ref, wgr_ref, gup_ref, gb_ref, cos_ref, sin_ref,
                   proj_ref, loga_ref, h_ref, *, q_tile, k_tile):
    j = pl.program_id(1)

    @pl.when(j == 0)
    def _():
        hb = _rms(x_ref[...], g_ref[...]).astype(BF16)
        h_ref[...] = hb
        gr = jnp.dot(hb, wgr_ref[...], preferred_element_type=F32)
        z = jnp.dot(gr.astype(BF16), gup_ref[...], preferred_element_type=F32) + gb_ref[...]
        log_sig = jnp.minimum(z, 0.0) - jnp.log1p(jnp.exp(-jnp.abs(z)))
        loga_ref[...] = log_sig / GLA_TAU

    is_rotary = (j == q_tile) | (j == k_tile)

    @pl.when(jnp.logical_not(is_rotary))
    def _():
        proj_ref[...] = jnp.dot(h_ref[...], w_ref[...], preferred_element_type=F32).astype(proj_ref.dtype)

    @pl.when(is_rotary)
    def _():
        p = jnp.dot(h_ref[...], w_ref[...], preferred_element_type=F32)
        scale = jnp.where(j == k_tile, RET_DK ** -0.5, 1.0).astype(F32)
        cos = cos_ref[...] * scale
        sin = sin_ref[...] * scale
        half = RET_DK // 2
        for h in range(RET_HEADS):
            lo = slice(h * RET_DK, h * RET_DK + half)
            hi = slice(h * RET_DK + half, (h + 1) * RET_DK)
            t1, t2 = p[:, lo], p[:, hi]
            proj_ref[:, lo] = (t1 * cos - t2 * sin).astype(proj_ref.dtype)
            proj_ref[:, hi] = (t2 * cos + t1 * sin).astype(proj_ref.dtype)


def _inproj(x, gain, w_main, w_gr, gate_up, gate_bias, cos, sin, layer, tm=1024):
    n_tok = x.shape[0]
    tn = RET_QK
    assert OFF_RQ % tn == 0 and OFF_RK % tn == 0
    return pl.pallas_call(
        functools.partial(_inproj_kernel, q_tile=OFF_RQ // tn, k_tile=OFF_RK // tn),
        grid=(n_tok // tm, D_PROJ // tn),
        in_specs=[pl.BlockSpec((tm, D_MODEL), lambda i, j: (i, 0)),
                  pl.BlockSpec((None, 1, D_MODEL), lambda i, j: (layer, 0, 0)),
                  pl.BlockSpec((None, D_MODEL, tn), lambda i, j: (layer, 0, j)),
                  pl.BlockSpec((None, D_MODEL, LANES), lambda i, j: (layer, 0, 0)),
                  pl.BlockSpec((None, LANES, GLA_QK), lambda i, j: (layer, 0, 0)),
                  pl.BlockSpec((None, 1, GLA_QK), lambda i, j: (layer, 0, 0)),
                  pl.BlockSpec((tm, RET_DK // 2), lambda i, j: (i, 0)),
                  pl.BlockSpec((tm, RET_DK // 2), lambda i, j: (i, 0))],
        out_specs=[pl.BlockSpec((tm, tn), lambda i, j: (i, j)),
                   pl.BlockSpec((tm, GLA_QK), lambda i, j: (i, 0))],
        out_shape=[jax.ShapeDtypeStruct((n_tok, D_PROJ), BF16),
                   jax.ShapeDtypeStruct((n_tok, GLA_QK), F32)],
        scratch_shapes=[pltpu.VMEM((tm, D_MODEL), BF16)],
        compiler_params=_params(2),
        name="norm_inproj",
    )(x, gain, w_main, w_gr, gate_up, gate_bias, cos, sin)


def _split3(x):
    hi = x.astype(BF16)
    r1 = x - hi.astype(F32)
    mid = r1.astype(BF16)
    lo = (r1 - mid.astype(F32)).astype(BF16)
    return hi, mid, lo


def _mixer_kernel(proj_ref, loga_ref, gno_ref, rgain_ref, rbias_ref, mix_ref, sg_ref, sr_ref):
    t_blk = MIX_BLOCK
    n_chunk = t_blk // GLA_CHUNK

    @pl.when(pl.program_id(1) == 0)
    def _():
        sg_ref[...] = jnp.zeros_like(sg_ref)
        sr_ref[...] = jnp.zeros_like(sr_ref)

    row = lax.broadcasted_iota(jnp.int32, (t_blk, t_blk), 0)
    col = lax.broadcasted_iota(jnp.int32, (t_blk, t_blk), 1)
    causal = col <= row
    chunk_shift = int(math.log2(GLA_CHUNK))
    gla_mask = causal & ((row >> chunk_shift) == (col >> chunk_shift))
    rel = jnp.maximum(row - col, 0).astype(F32)
    rowf = lax.broadcasted_iota(jnp.int32, (t_blk, RET_DV), 0).astype(F32)

    cum = gla_mask.astype(BF16)
    la_hi, la_mid, la_lo = _split3(loga_ref[...])
    b_all = (jnp.dot(cum, la_hi, preferred_element_type=F32)
             + jnp.dot(cum, la_mid, preferred_element_type=F32)
             + jnp.dot(cum, la_lo, preferred_element_type=F32))

    for h in range(GLA_HEADS):
        q = proj_ref[:, OFF_GQ + h * GLA_DK:OFF_GQ + (h + 1) * GLA_DK].astype(F32)
        k = proj_ref[:, OFF_GK + h * GLA_DK:OFF_GK + (h + 1) * GLA_DK].astype(F32)
        v = proj_ref[:, OFF_GV + h * GLA_DV:OFF_GV + (h + 1) * GLA_DV]
        g = proj_ref[:, OFF_GG + h * GLA_DV:OFF_GG + (h + 1) * GLA_DV].astype(F32)
        b = b_all[:, h * GLA_DK:(h + 1) * GLA_DK]
        b_last = [b[(c + 1) * GLA_CHUNK - 1:(c + 1) * GLA_CHUNK, :] for c in range(n_chunk)]
        b_last_rows = jnp.concatenate(
            [jnp.broadcast_to(bl, (GLA_CHUNK, GLA_DK)) for bl in b_last], axis=0)
        q_dec = (q * (GLA_DK ** -0.5) * jnp.exp(b)).astype(BF16)
        k_dec = (k * jnp.exp(-b)).astype(BF16)
        k_tail = (k * jnp.exp(b_last_rows - b)).astype(BF16)

        s = lax.dot_general(q_dec, k_dec, _NT, preferred_element_type=F32)
        s = jnp.where(gla_mask, s, 0.0).astype(BF16)
        o_intra = jnp.dot(s, v, preferred_element_type=F32)

        state = sg_ref[h]
        o_parts = []
        for c in range(n_chunk):
            sl = slice(c * GLA_CHUNK, (c + 1) * GLA_CHUNK)
            o_parts.append(o_intra[sl] + jnp.dot(q_dec[sl], state.astype(BF16),
                                                 preferred_element_type=F32))
            local = lax.dot_general(k_tail[sl], v[sl], _TN, preferred_element_type=F32)
            dcol = jnp.broadcast_to(jnp.exp(b_last[c]), (GLA_DK, GLA_DK)).T
            state = jnp.concatenate([dcol, dcol], axis=1) * state + local
        sg_ref[h] = state

        o = jnp.concatenate(o_parts, axis=0)
        y = o * lax.rsqrt(jnp.mean(o * o, axis=-1, keepdims=True) + EPS)
        y = y * gno_ref[:, h * GLA_DV:(h + 1) * GLA_DV]
        y = y * (g / (1.0 + jnp.exp(-g)))
        mix_ref[:, h * GLA_DV:(h + 1) * GLA_DV] = y.astype(mix_ref.dtype)

    for h in range(RET_HEADS):
        log_gamma = math.log1p(-(2.0 ** (-5.0 - h)))
        qb = proj_ref[:, OFF_RQ + h * RET_DK:OFF_RQ + (h + 1) * RET_DK]
        kb = proj_ref[:, OFF_RK + h * RET_DK:OFF_RK + (h + 1) * RET_DK]
        v = proj_ref[:, OFF_RV + h * RET_DV:OFF_RV + (h + 1) * RET_DV]
        g = proj_ref[:, OFF_RG + h * RET_DV:OFF_RG + (h + 1) * RET_DV].astype(F32)
        decay = jnp.where(causal, jnp.exp(log_gamma * rel), 0.0)
        s = lax.dot_general(qb, kb, _NT, preferred_element_type=F32) * decay
        o_intra = jnp.dot(s.astype(BF16), v, preferred_element_type=F32)
        k_tail = (kb.astype(F32) * jnp.exp(log_gamma * (t_blk - 1.0 - rowf))).astype(BF16)
        local = lax.dot_general(k_tail, v, _TN, preferred_element_type=F32)
        state = sr_ref[h]
        o_inter = jnp.dot(qb, state.astype(BF16), preferred_element_type=F32)
        o = o_intra + o_inter * jnp.exp(log_gamma * (rowf + 1.0))
        sr_ref[h] = math.exp(log_gamma * t_blk) * state + local

        mu = jnp.mean(o, axis=-1, keepdims=True)
        cen = o - mu
        y = cen * lax.rsqrt(jnp.mean(cen * cen, axis=-1, keepdims=True) + EPS)
        y = y * rgain_ref[:, h * RET_DV:(h + 1) * RET_DV] + rbias_ref[:, h * RET_DV:(h + 1) * RET_DV]
        y = y * (g / (1.0 + jnp.exp(-g)))
        mix_ref[:, GLA_V + h * RET_DV:GLA_V + (h + 1) * RET_DV] = y.astype(mix_ref.dtype)


def _mixers(proj, loga, gla_norm, ret_gain, ret_bias, layer, batch, seq):
    n_tok = proj.shape[0]
    t_blk = MIX_BLOCK
    nt = seq // t_blk
    tok = lambda b, t: (b * nt + t, 0)
    const = lambda b, t: (layer, 0, 0)
    return pl.pallas_call(
        _mixer_kernel,
        grid=(batch, nt),
        in_specs=[pl.BlockSpec((t_blk, D_PROJ), tok),
                  pl.BlockSpec((t_blk, GLA_QK), tok),
                  pl.BlockSpec((None, 1, GLA_V), const),
                  pl.BlockSpec((None, 1, RET_V), const),
                  pl.BlockSpec((None, 1, RET_V), const)],
        out_specs=pl.BlockSpec((t_blk, D_MODEL), tok),
        out_shape=jax.ShapeDtypeStruct((n_tok, D_MODEL), BF16),
        scratch_shapes=[pltpu.VMEM((GLA_HEADS, GLA_DK, GLA_DV), F32),
                        pltpu.VMEM((RET_HEADS, RET_DK, RET_DV), F32)],
        compiler_params=_params(2),
        name="mixers",
    )(proj, loga, gla_norm, ret_gain, ret_bias)


def _outproj_kernel(x_ref, mix_ref, w_ref, o_ref):
    o_ref[...] = x_ref[...] + jnp.dot(mix_ref[...], w_ref[...], preferred_element_type=F32)


def _outproj(x, mix, w_out, layer, tm=1024, tn=1024):
    n_tok = x.shape[0]
    return pl.pallas_call(
        _outproj_kernel,
        grid=(n_tok // tm, D_MODEL // tn),
        in_specs=[pl.BlockSpec((tm, tn), lambda i, j: (i, j)),
                  pl.BlockSpec((tm, D_MODEL), lambda i, j: (i, 0)),
                  pl.BlockSpec((None, D_MODEL, tn), lambda i, j: (layer, 0, j))],
        out_specs=pl.BlockSpec((tm, tn), lambda i, j: (i, j)),
        out_shape=jax.ShapeDtypeStruct((n_tok, D_MODEL), F32),
        compiler_params=_params(2),
        name="outproj_residual",
    )(x, mix, w_out)


def _mlp_kernel(x_ref, g_ref, wu_ref, wd_ref, fg_ref, o_ref, h_ref, *, apply_final_norm):
    k = pl.program_id(1)

    @pl.when(k == 0)
    def _():
        x = x_ref[...]
        h_ref[...] = _rms(x, g_ref[...]).astype(BF16)
        o_ref[...] = x

    u = jnp.dot(h_ref[...], wu_ref[...], preferred_element_type=F32)
    a = jnp.square(jnp.maximum(u, 0.0)).astype(BF16)
    o_ref[...] += jnp.dot(a, wd_ref[...], preferred_element_type=F32)

    if apply_final_norm:
        @pl.when(k == pl.num_programs(1) - 1)
        def _():
            o_ref[...] = _rms(o_ref[...], fg_ref[...])


def _mlp(x, gain, w_up, w_down, final_gain, layer, apply_final_norm, tm=1024, tf=512):
    n_tok = x.shape[0]
    return pl.pallas_call(
        functools.partial(_mlp_kernel, apply_final_norm=apply_final_norm),
        grid=(n_tok // tm, D_FF // tf),
        in_specs=[pl.BlockSpec((tm, D_MODEL), lambda i, k: (i, 0)),
                  pl.BlockSpec((None, 1, D_MODEL), lambda i, k: (layer, 0, 0)),
                  pl.BlockSpec((None, D_MODEL, tf), lambda i, k: (layer, 0, k)),
                  pl.BlockSpec((None, tf, D_MODEL), lambda i, k: (layer, k, 0)),
                  pl.BlockSpec((1, D_MODEL), lambda i, k: (0, 0))],
        out_specs=pl.BlockSpec((tm, D_MODEL), lambda i, k: (i, 0)),
        out_shape=jax.ShapeDtypeStruct((n_tok, D_MODEL), F32),
        scratch_shapes=[pltpu.VMEM((tm, D_MODEL), BF16)],
        compiler_params=_params(2),
        name="norm_mlp_residual",
    )(x, gain, w_up, w_down, final_gain)


def kernel(x, positions, attn_norm, w_in, gla_gate_up, gla_gate_bias, gla_out_norm,
           ret_norm_gain, ret_norm_bias, w_out, mlp_norm, w_up, w_down, final_norm):
    batch, seq, d_model = x.shape
    depth = w_in.shape[0]
    assert d_model == D_MODEL and seq % MIX_BLOCK == 0
    n_tok = batch * seq
    xf = x.reshape(n_tok, d_model)

    cos, sin = _rope_tables(positions)

    gr_lo = 2 * GLA_QK + 2 * GLA_V
    gr_hi = gr_lo + GLA_GATE_RANK
    pad = LANES - GLA_GATE_RANK
    w_main = jnp.concatenate([w_in[:, :, :gr_lo], w_in[:, :, gr_hi:]], axis=2).astype(BF16)
    w_gr = jnp.pad(w_in[:, :, gr_lo:gr_hi], ((0, 0), (0, 0), (0, pad))).astype(BF16)
    gate_up = jnp.pad(gla_gate_up, ((0, 0), (0, pad), (0, 0))).astype(BF16)
    w_out_b = w_out.astype(BF16)
    w_up_b = w_up.astype(BF16)
    w_down_b = w_down.astype(BF16)
    row3 = lambda a: a.reshape(depth, 1, -1)
    attn_g, gate_b, mlp_g = row3(attn_norm), row3(gla_gate_bias), row3(mlp_norm)
    gla_g, ret_g, ret_b = row3(gla_out_norm), row3(ret_norm_gain), row3(ret_norm_bias)
    final_g = final_norm.reshape(1, -1)

    for l in range(depth):
        proj, loga = _inproj(xf, attn_g, w_main, w_gr, gate_up, gate_b, cos, sin, l)
        mix = _mixers(proj, loga, gla_g, ret_g, ret_b, l, batch, seq)
        xf = _outproj(xf, mix, w_out_b, l)
        xf = _mlp(xf, mlp_g, w_up_b, w_down_b, final_g, l, apply_final_norm=(l == depth - 1))
    return xf.reshape(batch, seq, d_model)
```

```python
import functools
import math

import jax
import jax.numpy as jnp
from jax import lax
from jax.experimental import pallas as pl
from jax.experimental.pallas import tpu as pltpu

D_MODEL = 2048
GLA_HEADS = 4
GLA_DK = 128
GLA_DV = 256
GLA_GATE_RANK = 16
GLA_TAU = 16.0
GLA_CHUNK = 64
RET_HEADS = 4
RET_DK = 256
RET_DV = 256
ROPE_BASE = 10000.0
D_FF = 4 * D_MODEL
EPS = 1e-6

GLA_QK = GLA_HEADS * GLA_DK
GLA_V = GLA_HEADS * GLA_DV
RET_QK = RET_HEADS * RET_DK
RET_V = RET_HEADS * RET_DV
D_PROJ = 2 * GLA_QK + 2 * GLA_V + 2 * RET_QK + 2 * RET_V

OFF_GQ = 0
OFF_GK = OFF_GQ + GLA_QK
OFF_GV = OFF_GK + GLA_QK
OFF_GG = OFF_GV + GLA_V
OFF_RQ = OFF_GG + GLA_V
OFF_RK = OFF_RQ + RET_QK
OFF_RV = OFF_RK + RET_QK
OFF_RG = OFF_RV + RET_V

LANES = 128
MIX_BLOCK = 256
VMEM_LIMIT = 56 * 1024 * 1024

F32 = jnp.float32
BF16 = jnp.bfloat16

_NT = (((1,), (1,)), ((), ()))
_TN = (((0,), (0,)), ((), ()))


def _params(n_axes):
    return pltpu.CompilerParams(
        dimension_semantics=("arbitrary",) * n_axes, vmem_limit_bytes=VMEM_LIMIT)


def _rms(x, gain):
    return x * lax.rsqrt(jnp.mean(x * x, axis=-1, keepdims=True) + EPS) * gain


def _rope_kernel(pos_ref, invf_ref, cos_ref, sin_ref):
    rows = pos_ref.shape[0]
    invf = invf_ref[...]
    for r in range(rows):
        p = pos_ref[r:r + 1, :].astype(F32)
        col = jnp.broadcast_to(p, (LANES, LANES)).T
        ang = col * invf
        cos_ref[r * LANES:(r + 1) * LANES, :] = jnp.cos(ang)
        sin_ref[r * LANES:(r + 1) * LANES, :] = jnp.sin(ang)


def _rope_tables(positions):
    n_tok = positions.size
    rows = n_tok // LANES
    rb = 8
    pos2d = positions.reshape(rows, LANES)
    inv_freq = (ROPE_BASE ** (-jnp.arange(0, RET_DK, 2, dtype=F32) / RET_DK)).reshape(1, RET_DK // 2)
    out = jax.ShapeDtypeStruct((n_tok, RET_DK // 2), F32)
    return pl.pallas_call(
        _rope_kernel,
        grid=(rows // rb,),
        in_specs=[pl.BlockSpec((rb, LANES), lambda i: (i, 0)),
                  pl.BlockSpec((1, RET_DK // 2), lambda i: (0, 0))],
        out_specs=[pl.BlockSpec((rb * LANES, RET_DK // 2), lambda i: (i, 0))] * 2,
        out_shape=[out, out],
        compiler_params=_params(1),
        name="rope_tables",
    )(pos2d, inv_freq)


def _inproj_kernel(x_ref, g_ref, w_ref, wgr_ref, gup_ref, gb_ref, cos_ref, sin_ref,
                   proj_ref, loga_ref, h_ref, *, q_tile, k_tile):
    j = pl.program_id(1)

    @pl.when(j == 0)
    def _():
        hb = _rms(x_ref[...], g_ref[...]).astype(BF16)
        h_ref[...] = hb
        gr = jnp.dot(hb, wgr_ref[...], preferred_element_type=F32)
        z = jnp.dot(gr.astype(BF16), gup_ref[...], preferred_element_type=F32) + gb_ref[...]
        log_sig = jnp.minimum(z, 0.0) - jnp.log1p(jnp.exp(-jnp.abs(z)))
        loga_ref[...] = log_sig / GLA_TAU

    is_rotary = (j == q_tile) | (j == k_tile)

    @pl.when(jnp.logical_not(is_rotary))
    def _():
        proj_ref[...] = jnp.dot(h_ref[...], w_ref[...], preferred_element_type=F32).astype(proj_ref.dtype)

    @pl.when(is_rotary)
    def _():
        p = jnp.dot(h_ref[...], w_ref[...], preferred_element_type=F32)
        scale = jnp.where(j == k_tile, RET_DK ** -0.5, 1.0).astype(F32)
        cos = cos_ref[...] * scale
        sin = sin_ref[...] * scale
        half = RET_DK // 2
        for h in range(RET_HEADS):
            lo = slice(h * RET_DK, h * RET_DK + half)
            hi = slice(h * RET_DK + half, (h + 1) * RET_DK)
            t1, t2 = p[:, lo], p[:, hi]
            proj_ref[:, lo] = (t1 * cos - t2 * sin).astype(proj_ref.dtype)
            proj_ref[:, hi] = (t2 * cos + t1 * sin).astype(proj_ref.dtype)


def _inproj(x, gain, w_main, w_gr, gate_up, gate_bias, cos, sin, layer, tm=1024):
    n_tok = x.shape[0]
    tn = RET_QK
    assert OFF_RQ % tn == 0 and OFF_RK % tn == 0
    return pl.pallas_call(
        functools.partial(_inproj_kernel, q_tile=OFF_RQ // tn, k_tile=OFF_RK // tn),
        grid=(n_tok // tm, D_PROJ // tn),
        in_specs=[pl.BlockSpec((tm, D_MODEL), lambda i, j: (i, 0)),
                  pl.BlockSpec((None, 1, D_MODEL), lambda i, j: (layer, 0, 0)),
                  pl.BlockSpec((None, D_MODEL, tn), lambda i, j: (layer, 0, j)),
                  pl.BlockSpec((None, D_MODEL, LANES), lambda i, j: (layer, 0, 0)),
                  pl.BlockSpec((None, LANES, GLA_QK), lambda i, j: (layer, 0, 0)),
                  pl.BlockSpec((None, 1, GLA_QK), lambda i, j: (layer, 0, 0)),
                  pl.BlockSpec((tm, RET_DK // 2), lambda i, j: (i, 0)),
                  pl.BlockSpec((tm, RET_DK // 2), lambda i, j: (i, 0))],
        out_specs=[pl.BlockSpec((tm, tn), lambda i, j: (i, j)),
                   pl.BlockSpec((tm, GLA_QK), lambda i, j: (i, 0))],
        out_shape=[jax.ShapeDtypeStruct((n_tok, D_PROJ), BF16),
                   jax.ShapeDtypeStruct((n_tok, GLA_QK), F32)],
        scratch_shapes=[pltpu.VMEM((tm, D_MODEL), BF16)],
        compiler_params=_params(2),
        name="norm_inproj",
    )(x, gain, w_main, w_gr, gate_up, gate_bias, cos, sin)


def _split3(x):
    hi = x.astype(BF16)
    r1 = x - hi.astype(F32)
    mid = r1.astype(BF16)
    lo = (r1 - mid.astype(F32)).astype(BF16)
    return hi, mid, lo


def _mix_block(proj_ref, loga_ref, gno_ref, rgain_ref, rbias_ref, mix_ref, sg_ref, sr_ref):
    t_blk = MIX_BLOCK
    n_chunk = t_blk // GLA_CHUNK

    row = lax.broadcasted_iota(jnp.int32, (t_blk, t_blk), 0)
    col = lax.broadcasted_iota(jnp.int32, (t_blk, t_blk), 1)
    causal = col <= row
    chunk_shift = int(math.log2(GLA_CHUNK))
    gla_mask = causal & ((row >> chunk_shift) == (col >> chunk_shift))
    rel = jnp.maximum(row - col, 0).astype(F32)
    rowf = lax.broadcasted_iota(jnp.int32, (t_blk, RET_DV), 0).astype(F32)

    cum = gla_mask.astype(BF16)
    la_hi, la_mid, la_lo = _split3(loga_ref[...])
    b_all = (jnp.dot(cum, la_hi, preferred_element_type=F32)
             + jnp.dot(cum, la_mid, preferred_element_type=F32)
             + jnp.dot(cum, la_lo, preferred_element_type=F32))

    for h in range(GLA_HEADS):
        q = proj_ref[:, OFF_GQ + h * GLA_DK:OFF_GQ + (h + 1) * GLA_DK].astype(F32)
        k = proj_ref[:, OFF_GK + h * GLA_DK:OFF_GK + (h + 1) * GLA_DK].astype(F32)
        v = proj_ref[:, OFF_GV + h * GLA_DV:OFF_GV + (h + 1) * GLA_DV]
        g = proj_ref[:, OFF_GG + h * GLA_DV:OFF_GG + (h + 1) * GLA_DV].astype(F32)
        b = b_all[:, h * GLA_DK:(h + 1) * GLA_DK]
        b_last = [b[(c + 1) * GLA_CHUNK - 1:(c + 1) * GLA_CHUNK, :] for c in range(n_chunk)]
        b_last_rows = jnp.concatenate(
            [jnp.broadcast_to(bl, (GLA_CHUNK, GLA_DK)) for bl in b_last], axis=0)
        q_dec = (q * (GLA_DK ** -0.5) * jnp.exp(b)).astype(BF16)
        k_dec = (k * jnp.exp(-b)).astype(BF16)
        k_tail = (k * jnp.exp(b_last_rows - b)).astype(BF16)

        s = lax.dot_general(q_dec, k_dec, _NT, preferred_element_type=F32)
        s = jnp.where(gla_mask, s, 0.0).astype(BF16)
        o_intra = jnp.dot(s, v, preferred_element_type=F32)

        state = sg_ref[h]
        o_parts = []
        for c in range(n_chunk):
            sl = slice(c * GLA_CHUNK, (c + 1) * GLA_CHUNK)
            o_parts.append(o_intra[sl] + jnp.dot(q_dec[sl], state.astype(BF16),
                                                 preferred_element_type=F32))
            local = lax.dot_general(k_tail[sl], v[sl], _TN, preferred_element_type=F32)
            dcol = jnp.broadcast_to(jnp.exp(b_last[c]), (GLA_DK, GLA_DK)).T
            state = jnp.concatenate([dcol, dcol], axis=1) * state + local
        sg_ref[h] = state

        o = jnp.concatenate(o_parts, axis=0)
        y = o * lax.rsqrt(jnp.mean(o * o, axis=-1, keepdims=True) + EPS)
        y = y * gno_ref[:, h * GLA_DV:(h + 1) * GLA_DV]
        y = y * (g / (1.0 + jnp.exp(-g)))
        mix_ref[:, h * GLA_DV:(h + 1) * GLA_DV] = y.astype(mix_ref.dtype)

    for h in range(RET_HEADS):
        log_gamma = math.log1p(-(2.0 ** (-5.0 - h)))
        qb = proj_ref[:, OFF_RQ + h * RET_DK:OFF_RQ + (h + 1) * RET_DK]
        kb = proj_ref[:, OFF_RK + h * RET_DK:OFF_RK + (h + 1) * RET_DK]
        v = proj_ref[:, OFF_RV + h * RET_DV:OFF_RV + (h + 1) * RET_DV]
        g = proj_ref[:, OFF_RG + h * RET_DV:OFF_RG + (h + 1) * RET_DV].astype(F32)
        decay = jnp.where(causal, jnp.exp(log_gamma * rel), 0.0)
        s = lax.dot_general(qb, kb, _NT, preferred_element_type=F32) * decay
        o_intra = jnp.dot(s.astype(BF16), v, preferred_element_type=F32)
        k_tail = (kb.astype(F32) * jnp.exp(log_gamma * (t_blk - 1.0 - rowf))).astype(BF16)
        local = lax.dot_general(k_tail, v, _TN, preferred_element_type=F32)
        state = sr_ref[h]
        o_inter = jnp.dot(qb, state.astype(BF16), preferred_element_type=F32)
        o = o_intra + o_inter * jnp.exp(log_gamma * (rowf + 1.0))
        sr_ref[h] = math.exp(log_gamma * t_blk) * state + local

        mu = jnp.mean(o, axis=-1, keepdims=True)
        cen = o - mu
        y = cen * lax.rsqrt(jnp.mean(cen * cen, axis=-1, keepdims=True) + EPS)
        y = y * rgain_ref[:, h * RET_DV:(h + 1) * RET_DV] + rbias_ref[:, h * RET_DV:(h + 1) * RET_DV]
        y = y * (g / (1.0 + jnp.exp(-g)))
        mix_ref[:, GLA_V + h * RET_DV:GLA_V + (h + 1) * RET_DV] = y.astype(mix_ref.dtype)


def _mixer_outproj_kernel(proj_ref, loga_ref, gno_ref, rgain_ref, rbias_ref, x_ref, wout_ref,
                          out_ref, sg_ref, sr_ref, mix_a, mix_b, *, blocks_per_seq):
    t = pl.program_id(0)

    @pl.when(lax.rem(t, blocks_per_seq) == 0)
    def _():
        sg_ref[...] = jnp.zeros_like(sg_ref)
        sr_ref[...] = jnp.zeros_like(sr_ref)

    @pl.when(t == 0)
    def _():
        mix_b[...] = jnp.zeros_like(mix_b)

    def step(src, dst):
        _mix_block(proj_ref, loga_ref, gno_ref, rgain_ref, rbias_ref, dst, sg_ref, sr_ref)
        out_ref[...] = x_ref[...] + jnp.dot(src[...], wout_ref[...], preferred_element_type=F32)

    parity = lax.rem(t, 2)

    @pl.when(parity == 0)
    def _():
        step(mix_b, mix_a)

    @pl.when(parity == 1)
    def _():
        step(mix_a, mix_b)


def _mixers_outproj(proj, loga, gla_norm, ret_gain, ret_bias, x, w_out, layer, seq):
    n_tok = proj.shape[0]
    t_blk = MIX_BLOCK
    n_blk = n_tok // t_blk
    cur = lambda t: (jnp.minimum(t, n_blk - 1), 0)
    prev = lambda t: (jnp.maximum(t - 1, 0), 0)
    const = lambda t: (layer, 0, 0)
    return pl.pallas_call(
        functools.partial(_mixer_outproj_kernel, blocks_per_seq=seq // t_blk),
        grid=(n_blk + 1,),
        in_specs=[pl.BlockSpec((t_blk, D_PROJ), cur),
                  pl.BlockSpec((t_blk, GLA_QK), cur),
                  pl.BlockSpec((None, 1, GLA_V), const),
                  pl.BlockSpec((None, 1, RET_V), const),
                  pl.BlockSpec((None, 1, RET_V), const),
                  pl.BlockSpec((t_blk, D_MODEL), prev),
                  pl.BlockSpec((None, D_MODEL, D_MODEL), const)],
        out_specs=pl.BlockSpec((t_blk, D_MODEL), prev),
        out_shape=jax.ShapeDtypeStruct((n_tok, D_MODEL), F32),
        scratch_shapes=[pltpu.VMEM((GLA_HEADS, GLA_DK, GLA_DV), F32),
                        pltpu.VMEM((RET_HEADS, RET_DK, RET_DV), F32),
                        pltpu.VMEM((t_blk, D_MODEL), BF16),
                        pltpu.VMEM((t_blk, D_MODEL), BF16)],
        compiler_params=_params(1),
        name="mixers_outproj",
    )(proj, loga, gla_norm, ret_gain, ret_bias, x, w_out)


def _mlp_kernel(x_ref, g_ref, wu_ref, wd_ref, fg_ref, o_ref, h_ref, *, apply_final_norm):
    k = pl.program_id(1)

    @pl.when(k == 0)
    def _():
        x = x_ref[...]
        h_ref[...] = _rms(x, g_ref[...]).astype(BF16)
        o_ref[...] = x

    u = jnp.dot(h_ref[...], wu_ref[...], preferred_element_type=F32)
    a = jnp.square(jnp.maximum(u, 0.0)).astype(BF16)
    o_ref[...] += jnp.dot(a, wd_ref[...], preferred_element_type=F32)

    if apply_final_norm:
        @pl.when(k == pl.num_programs(1) - 1)
        def _():
            o_ref[...] = _rms(o_ref[...], fg_ref[...])


def _mlp(x, gain, w_up, w_down, final_gain, layer, apply_final_norm, tm=1024, tf=512):
    n_tok = x.shape[0]
    return pl.pallas_call(
        functools.partial(_mlp_kernel, apply_final_norm=apply_final_norm),
        grid=(n_tok // tm, D_FF // tf),
        in_specs=[pl.BlockSpec((tm, D_MODEL), lambda i, k: (i, 0)),
                  pl.BlockSpec((None, 1, D_MODEL), lambda i, k: (layer, 0, 0)),
                  pl.BlockSpec((None, D_MODEL, tf), lambda i, k: (layer, 0, k)),
                  pl.BlockSpec((None, tf, D_MODEL), lambda i, k: (layer, k, 0)),
                  pl.BlockSpec((1, D_MODEL), lambda i, k: (0, 0))],
        out_specs=pl.BlockSpec((tm, D_MODEL), lambda i, k: (i, 0)),
        out_shape=jax.ShapeDtypeStruct((n_tok, D_MODEL), F32),
        scratch_shapes=[pltpu.VMEM((tm, D_MODEL), BF16)],
        compiler_params=_params(2),
        name="norm_mlp_residual",
    )(x, gain, w_up, w_down, final_gain)


def kernel(x, positions, attn_norm, w_in, gla_gate_up, gla_gate_bias, gla_out_norm,
           ret_norm_gain, ret_norm_bias, w_out, mlp_norm, w_up, w_down, final_norm):
    batch, seq, d_model = x.shape
    depth = w_in.shape[0]
    assert d_model == D_MODEL and seq % MIX_BLOCK == 0
    n_tok = batch * seq
    xf = x.reshape(n_tok, d_model)

    cos, sin = _rope_tables(positions)

    gr_lo = 2 * GLA_QK + 2 * GLA_V
    gr_hi = gr_lo + GLA_GATE_RANK
    pad = LANES - GLA_GATE_RANK
    w_main = jnp.concatenate([w_in[:, :, :gr_lo], w_in[:, :, gr_hi:]], axis=2).astype(BF16)
    w_gr = jnp.pad(w_in[:, :, gr_lo:gr_hi], ((0, 0), (0, 0), (0, pad))).astype(BF16)
    gate_up = jnp.pad(gla_gate_up, ((0, 0), (0, pad), (0, 0))).astype(BF16)
    w_out_b = w_out.astype(BF16)
    w_up_b = w_up.astype(BF16)
    w_down_b = w_down.astype(BF16)
    row3 = lambda a: a.reshape(depth, 1, -1)
    attn_g, gate_b, mlp_g = row3(attn_norm), row3(gla_gate_bias), row3(mlp_norm)
    gla_g, ret_g, ret_b = row3(gla_out_norm), row3(ret_norm_gain), row3(ret_norm_bias)
    final_g = final_norm.reshape(1, -1)

    for l in range(depth):
        proj, loga = _inproj(xf, attn_g, w_main, w_gr, gate_up, gate_b, cos, sin, l)
        xf = _mixers_outproj(proj, loga, gla_g, ret_g, ret_b, xf, w_out_b, l, seq)
        xf = _mlp(xf, mlp_g, w_up_b, w_down_b, final_g, l, apply_final_norm=(l == depth - 1))
    return xf.reshape(batch, seq, d_model)
```

```python
import functools
import math

import jax
import jax.numpy as jnp
from jax import lax
from jax.experimental import pallas as pl
from jax.experimental.pallas import tpu as pltpu

D_MODEL = 2048
GLA_HEADS = 4
GLA_DK = 128
GLA_DV = 256
GLA_GATE_RANK = 16
GLA_TAU = 16.0
GLA_CHUNK = 64
RET_HEADS = 4
RET_DK = 256
RET_DV = 256
ROPE_BASE = 10000.0
D_FF = 4 * D_MODEL
EPS = 1e-6

GLA_QK = GLA_HEADS * GLA_DK
GLA_V = GLA_HEADS * GLA_DV
RET_QK = RET_HEADS * RET_DK
RET_V = RET_HEADS * RET_DV
D_PROJ = 2 * GLA_QK + 2 * GLA_V + 2 * RET_QK + 2 * RET_V

OFF_GQ = 0
OFF_GK = OFF_GQ + GLA_QK
OFF_GV = OFF_GK + GLA_QK
OFF_GG = OFF_GV + GLA_V
OFF_RQ = OFF_GG + GLA_V
OFF_RK = OFF_RQ + RET_QK
OFF_RV = OFF_RK + RET_QK
OFF_RG = OFF_RV + RET_V

LANES = 128
MIX_BLOCK = 256
VMEM_LIMIT = 60 * 1024 * 1024

F32 = jnp.float32
BF16 = jnp.bfloat16

_NT = (((1,), (1,)), ((), ()))
_TN = (((0,), (0,)), ((), ()))


def _params(n_axes):
    return pltpu.CompilerParams(
        dimension_semantics=("arbitrary",) * n_axes, vmem_limit_bytes=VMEM_LIMIT)


def _rms(x, gain):
    return x * lax.rsqrt(jnp.mean(x * x, axis=-1, keepdims=True) + EPS) * gain


def _rope_kernel(pos_ref, invf_ref, cos_ref, sin_ref):
    rows = pos_ref.shape[0]
    invf = invf_ref[...]
    for r in range(rows):
        p = pos_ref[r:r + 1, :].astype(F32)
        col = jnp.broadcast_to(p, (LANES, LANES)).T
        ang = col * invf
        cos_ref[r * LANES:(r + 1) * LANES, :] = jnp.cos(ang)
        sin_ref[r * LANES:(r + 1) * LANES, :] = jnp.sin(ang)


def _rope_tables(positions):
    n_tok = positions.size
    rows = n_tok // LANES
    rb = 8
    pos2d = positions.reshape(rows, LANES)
    inv_freq = (ROPE_BASE ** (-jnp.arange(0, RET_DK, 2, dtype=F32) / RET_DK)).reshape(1, RET_DK // 2)
    out = jax.ShapeDtypeStruct((n_tok, RET_DK // 2), F32)
    return pl.pallas_call(
        _rope_kernel,
        grid=(rows // rb,),
        in_specs=[pl.BlockSpec((rb, LANES), lambda i: (i, 0)),
                  pl.BlockSpec((1, RET_DK // 2), lambda i: (0, 0))],
        out_specs=[pl.BlockSpec((rb * LANES, RET_DK // 2), lambda i: (i, 0))] * 2,
        out_shape=[out, out],
        compiler_params=_params(1),
        name="rope_tables",
    )(pos2d, inv_freq)


GATE_LO = 2 * GLA_QK + 2 * GLA_V
GATE_HI = GATE_LO + GLA_GATE_RANK


def _win_cast_kernel(w_ref, main_ref, gr_ref):
    main_ref[:, :GATE_LO] = w_ref[:, :GATE_LO].astype(main_ref.dtype)
    main_ref[:, GATE_LO:] = w_ref[:, GATE_HI:].astype(main_ref.dtype)
    gr_ref[...] = jnp.zeros_like(gr_ref)
    gr_ref[:, :GLA_GATE_RANK] = w_ref[:, GATE_LO:GATE_HI].astype(gr_ref.dtype)


def _win_cast(w_in, rows=256):
    depth, d_model, d_in = w_in.shape
    return pl.pallas_call(
        _win_cast_kernel,
        grid=(depth, d_model // rows),
        in_specs=[pl.BlockSpec((None, rows, d_in), lambda l, r: (l, r, 0))],
        out_specs=[pl.BlockSpec((None, rows, D_PROJ), lambda l, r: (l, r, 0)),
                   pl.BlockSpec((None, rows, LANES), lambda l, r: (l, r, 0))],
        out_shape=[jax.ShapeDtypeStruct((depth, d_model, D_PROJ), BF16),
                   jax.ShapeDtypeStruct((depth, d_model, LANES), BF16)],
        compiler_params=_params(2),
        name="w_in_cast",
    )(w_in)


def _inproj_kernel(x_ref, g_ref, w_ref, wgr_ref, gup_ref, gb_ref, cos_ref, sin_ref,
                   proj_ref, loga_ref, h_ref, *, q_tile, k_tile):
    j = pl.program_id(1)

    @pl.when(j == 0)
    def _():
        hb = _rms(x_ref[...], g_ref[...]).astype(BF16)
        h_ref[...] = hb
        gr = jnp.dot(hb, wgr_ref[...], preferred_element_type=F32)
        z = jnp.dot(gr.astype(BF16), gup_ref[...], preferred_element_type=F32) + gb_ref[...]
        log_sig = jnp.minimum(z, 0.0) - jnp.log1p(jnp.exp(-jnp.abs(z)))
        loga_ref[...] = log_sig / GLA_TAU

    is_rotary = (j == q_tile) | (j == k_tile)

    @pl.when(jnp.logical_not(is_rotary))
    def _():
        proj_ref[...] = jnp.dot(h_ref[...], w_ref[...], preferred_element_type=F32).astype(proj_ref.dtype)

    @pl.when(is_rotary)
    def _():
        p = jnp.dot(h_ref[...], w_ref[...], preferred_element_type=F32)
        scale = jnp.where(j == k_tile, RET_DK ** -0.5, 1.0).astype(F32)
        cos = cos_ref[...] * scale
        sin = sin_ref[...] * scale
        half = RET_DK // 2
        for h in range(RET_HEADS):
            lo = slice(h * RET_DK, h * RET_DK + half)
            hi = slice(h * RET_DK + half, (h + 1) * RET_DK)
            t1, t2 = p[:, lo], p[:, hi]
            proj_ref[:, lo] = (t1 * cos - t2 * sin).astype(proj_ref.dtype)
            proj_ref[:, hi] = (t2 * cos + t1 * sin).astype(proj_ref.dtype)


def _inproj(x, gain, w_main, w_gr, gate_up, gate_bias, cos, sin, layer, tm=1024):
    n_tok = x.shape[0]
    tn = RET_QK
    assert OFF_RQ % tn == 0 and OFF_RK % tn == 0
    return pl.pallas_call(
        functools.partial(_inproj_kernel, q_tile=OFF_RQ // tn, k_tile=OFF_RK // tn),
        grid=(n_tok // tm, D_PROJ // tn),
        in_specs=[pl.BlockSpec((tm, D_MODEL), lambda i, j: (i, 0)),
                  pl.BlockSpec((None, 1, D_MODEL), lambda i, j: (layer, 0, 0)),
                  pl.BlockSpec((None, D_MODEL, tn), lambda i, j: (layer, 0, j)),
                  pl.BlockSpec((None, D_MODEL, LANES), lambda i, j: (layer, 0, 0)),
                  pl.BlockSpec((None, LANES, GLA_QK), lambda i, j: (layer, 0, 0)),
                  pl.BlockSpec((None, 1, GLA_QK), lambda i, j: (layer, 0, 0)),
                  pl.BlockSpec((tm, RET_DK // 2), lambda i, j: (i, 0)),
                  pl.BlockSpec((tm, RET_DK // 2), lambda i, j: (i, 0))],
        out_specs=[pl.BlockSpec((tm, tn), lambda i, j: (i, j)),
                   pl.BlockSpec((tm, GLA_QK), lambda i, j: (i, 0))],
        out_shape=[jax.ShapeDtypeStruct((n_tok, D_PROJ), BF16),
                   jax.ShapeDtypeStruct((n_tok, GLA_QK), F32)],
        scratch_shapes=[pltpu.VMEM((tm, D_MODEL), BF16)],
        compiler_params=_params(2),
        name="norm_inproj",
    )(x, gain, w_main, w_gr, gate_up, gate_bias, cos, sin)


def _split3(x):
    hi = x.astype(BF16)
    r1 = x - hi.astype(F32)
    mid = r1.astype(BF16)
    lo = (r1 - mid.astype(F32)).astype(BF16)
    return hi, mid, lo


def _mix_block(proj_ref, loga_ref, gno_ref, rgain_ref, rbias_ref, mix_ref, sg_ref, sr_ref):
    t_blk = MIX_BLOCK
    n_chunk = t_blk // GLA_CHUNK

    row = lax.broadcasted_iota(jnp.int32, (t_blk, t_blk), 0)
    col = lax.broadcasted_iota(jnp.int32, (t_blk, t_blk), 1)
    causal = col <= row
    chunk_shift = int(math.log2(GLA_CHUNK))
    gla_mask = causal & ((row >> chunk_shift) == (col >> chunk_shift))
    rel = jnp.maximum(row - col, 0).astype(F32)
    rowf = lax.broadcasted_iota(jnp.int32, (t_blk, RET_DV), 0).astype(F32)

    cum = gla_mask.astype(BF16)
    la_hi, la_mid, la_lo = _split3(loga_ref[...])
    b_all = (jnp.dot(cum, la_hi, preferred_element_type=F32)
             + jnp.dot(cum, la_mid, preferred_element_type=F32)
             + jnp.dot(cum, la_lo, preferred_element_type=F32))

    for h in range(GLA_HEADS):
        q = proj_ref[:, OFF_GQ + h * GLA_DK:OFF_GQ + (h + 1) * GLA_DK].astype(F32)
        k = proj_ref[:, OFF_GK + h * GLA_DK:OFF_GK + (h + 1) * GLA_DK].astype(F32)
        v = proj_ref[:, OFF_GV + h * GLA_DV:OFF_GV + (h + 1) * GLA_DV]
        g = proj_ref[:, OFF_GG + h * GLA_DV:OFF_GG + (h + 1) * GLA_DV].astype(F32)
        b = b_all[:, h * GLA_DK:(h + 1) * GLA_DK]
        b_last = [b[(c + 1) * GLA_CHUNK - 1:(c + 1) * GLA_CHUNK, :] for c in range(n_chunk)]
        b_last_rows = jnp.concatenate(
            [jnp.broadcast_to(bl, (GLA_CHUNK, GLA_DK)) for bl in b_last], axis=0)
        q_dec = (q * (GLA_DK ** -0.5) * jnp.exp(b)).astype(BF16)
        k_dec = (k * jnp.exp(-b)).astype(BF16)
        k_tail = (k * jnp.exp(b_last_rows - b)).astype(BF16)

        s = lax.dot_general(q_dec, k_dec, _NT, preferred_element_type=F32)
        s = jnp.where(gla_mask, s, 0.0).astype(BF16)
        o_intra = jnp.dot(s, v, preferred_element_type=F32)

        state = sg_ref[h]
        o_parts = []
        for c in range(n_chunk):
            sl = slice(c * GLA_CHUNK, (c + 1) * GLA_CHUNK)
            o_parts.append(o_intra[sl] + jnp.dot(q_dec[sl], state.astype(BF16),
                                                 preferred_element_type=F32))
            local = lax.dot_general(k_tail[sl], v[sl], _TN, preferred_element_type=F32)
            dcol = jnp.broadcast_to(jnp.exp(b_last[c]), (GLA_DK, GLA_DK)).T
            state = jnp.concatenate([dcol, dcol], axis=1) * state + local
        sg_ref[h] = state

        o = jnp.concatenate(o_parts, axis=0)
        y = o * lax.rsqrt(jnp.mean(o * o, axis=-1, keepdims=True) + EPS)
        y = y * gno_ref[:, h * GLA_DV:(h + 1) * GLA_DV]
        y = y * (g / (1.0 + jnp.exp(-g)))
        mix_ref[:, h * GLA_DV:(h + 1) * GLA_DV] = y.astype(mix_ref.dtype)

    for h in range(RET_HEADS):
        log_gamma = math.log1p(-(2.0 ** (-5.0 - h)))
        qb = proj_ref[:, OFF_RQ + h * RET_DK:OFF_RQ + (h + 1) * RET_DK]
        kb = proj_ref[:, OFF_RK + h * RET_DK:OFF_RK + (h + 1) * RET_DK]
        v = proj_ref[:, OFF_RV + h * RET_DV:OFF_RV + (h + 1) * RET_DV]
        g = proj_ref[:, OFF_RG + h * RET_DV:OFF_RG + (h + 1) * RET_DV].astype(F32)
        decay = jnp.where(causal, jnp.exp(log_gamma * rel), 0.0)
        s = lax.dot_general(qb, kb, _NT, preferred_element_type=F32) * decay
        o_intra = jnp.dot(s.astype(BF16), v, preferred_element_type=F32)
        k_tail = (kb.astype(F32) * jnp.exp(log_gamma * (t_blk - 1.0 - rowf))).astype(BF16)
        local = lax.dot_general(k_tail, v, _TN, preferred_element_type=F32)
        state = sr_ref[h]
        o_inter = jnp.dot(qb, state.astype(BF16), preferred_element_type=F32)
        o = o_intra + o_inter * jnp.exp(log_gamma * (rowf + 1.0))
        sr_ref[h] = math.exp(log_gamma * t_blk) * state + local

        mu = jnp.mean(o, axis=-1, keepdims=True)
        cen = o - mu
        y = cen * lax.rsqrt(jnp.mean(cen * cen, axis=-1, keepdims=True) + EPS)
        y = y * rgain_ref[:, h * RET_DV:(h + 1) * RET_DV] + rbias_ref[:, h * RET_DV:(h + 1) * RET_DV]
        y = y * (g / (1.0 + jnp.exp(-g)))
        mix_ref[:, GLA_V + h * RET_DV:GLA_V + (h + 1) * RET_DV] = y.astype(mix_ref.dtype)


def _mixer_outproj_kernel(proj_ref, loga_ref, gno_ref, rgain_ref, rbias_ref, x_ref, wout_ref,
                          out_ref, sg_ref, sr_ref, mix_a, mix_b, *, blocks_per_seq):
    t = pl.program_id(0)

    @pl.when(lax.rem(t, blocks_per_seq) == 0)
    def _():
        sg_ref[...] = jnp.zeros_like(sg_ref)
        sr_ref[...] = jnp.zeros_like(sr_ref)

    @pl.when(t == 0)
    def _():
        mix_b[...] = jnp.zeros_like(mix_b)

    def step(src, dst):
        _mix_block(proj_ref, loga_ref, gno_ref, rgain_ref, rbias_ref, dst, sg_ref, sr_ref)
        out_ref[...] = x_ref[...] + jnp.dot(src[...], wout_ref[...], preferred_element_type=F32)

    parity = lax.rem(t, 2)

    @pl.when(parity == 0)
    def _():
        step(mix_b, mix_a)

    @pl.when(parity == 1)
    def _():
        step(mix_a, mix_b)


def _mixers_outproj(proj, loga, gla_norm, ret_gain, ret_bias, x, w_out, layer, seq):
    n_tok = proj.shape[0]
    t_blk = MIX_BLOCK
    n_blk = n_tok // t_blk
    cur = lambda t: (jnp.minimum(t, n_blk - 1), 0)
    prev = lambda t: (jnp.maximum(t - 1, 0), 0)
    const = lambda t: (layer, 0, 0)
    return pl.pallas_call(
        functools.partial(_mixer_outproj_kernel, blocks_per_seq=seq // t_blk),
        grid=(n_blk + 1,),
        in_specs=[pl.BlockSpec((t_blk, D_PROJ), cur),
                  pl.BlockSpec((t_blk, GLA_QK), cur),
                  pl.BlockSpec((None, 1, GLA_V), const),
                  pl.BlockSpec((None, 1, RET_V), const),
                  pl.BlockSpec((None, 1, RET_V), const),
                  pl.BlockSpec((t_blk, D_MODEL), prev),
                  pl.BlockSpec((None, D_MODEL, D_MODEL), const)],
        out_specs=pl.BlockSpec((t_blk, D_MODEL), prev),
        out_shape=jax.ShapeDtypeStruct((n_tok, D_MODEL), F32),
        scratch_shapes=[pltpu.VMEM((GLA_HEADS, GLA_DK, GLA_DV), F32),
                        pltpu.VMEM((RET_HEADS, RET_DK, RET_DV), F32),
                        pltpu.VMEM((t_blk, D_MODEL), BF16),
                        pltpu.VMEM((t_blk, D_MODEL), BF16)],
        compiler_params=_params(1),
        name="mixers_outproj",
    )(proj, loga, gla_norm, ret_gain, ret_bias, x, w_out)


def _mlp_kernel(x_ref, g_ref, wu_ref, wd_ref, fg_ref, o_ref, h_ref, *, apply_final_norm):
    k = pl.program_id(1)

    @pl.when(k == 0)
    def _():
        x = x_ref[...]
        h_ref[...] = _rms(x, g_ref[...]).astype(BF16)
        o_ref[...] = x

    u = jnp.dot(h_ref[...], wu_ref[...], preferred_element_type=F32)
    a = jnp.square(jnp.maximum(u, 0.0)).astype(BF16)
    o_ref[...] += jnp.dot(a, wd_ref[...], preferred_element_type=F32)

    if apply_final_norm:
        @pl.when(k == pl.num_programs(1) - 1)
        def _():
            o_ref[...] = _rms(o_ref[...], fg_ref[...])


def _mlp(x, gain, w_up, w_down, final_gain, layer, apply_final_norm, tm=1024, tf=1024):
    n_tok = x.shape[0]
    return pl.pallas_call(
        functools.partial(_mlp_kernel, apply_final_norm=apply_final_norm),
        grid=(n_tok // tm, D_FF // tf),
        in_specs=[pl.BlockSpec((tm, D_MODEL), lambda i, k: (i, 0)),
                  pl.BlockSpec((None, 1, D_MODEL), lambda i, k: (layer, 0, 0)),
                  pl.BlockSpec((None, D_MODEL, tf), lambda i, k: (layer, 0, k)),
                  pl.BlockSpec((None, tf, D_MODEL), lambda i, k: (layer, k, 0)),
                  pl.BlockSpec((1, D_MODEL), lambda i, k: (0, 0))],
        out_specs=pl.BlockSpec((tm, D_MODEL), lambda i, k: (i, 0)),
        out_shape=jax.ShapeDtypeStruct((n_tok, D_MODEL), F32),
        scratch_shapes=[pltpu.VMEM((tm, D_MODEL), BF16)],
        compiler_params=_params(2),
        name="norm_mlp_residual",
    )(x, gain, w_up, w_down, final_gain)


def kernel(x, positions, attn_norm, w_in, gla_gate_up, gla_gate_bias, gla_out_norm,
           ret_norm_gain, ret_norm_bias, w_out, mlp_norm, w_up, w_down, final_norm):
    batch, seq, d_model = x.shape
    depth = w_in.shape[0]
    assert d_model == D_MODEL and seq % MIX_BLOCK == 0
    n_tok = batch * seq
    xf = x.reshape(n_tok, d_model)

    cos, sin = _rope_tables(positions)

    w_main, w_gr = _win_cast(w_in)
    gate_up = jnp.pad(gla_gate_up, ((0, 0), (0, LANES - GLA_GATE_RANK), (0, 0))).astype(BF16)
    w_out_b = w_out.astype(BF16)
    w_up_b = w_up.astype(BF16)
    w_down_b = w_down.astype(BF16)
    row3 = lambda a: a.reshape(depth, 1, -1)
    attn_g, gate_b, mlp_g = row3(attn_norm), row3(gla_gate_bias), row3(mlp_norm)
    gla_g, ret_g, ret_b = row3(gla_out_norm), row3(ret_norm_gain), row3(ret_norm_bias)
    final_g = final_norm.reshape(1, -1)

    for l in range(depth):
        proj, loga = _inproj(xf, attn_g, w_main, w_gr, gate_up, gate_b, cos, sin, l)
        xf = _mixers_outproj(proj, loga, gla_g, ret_g, ret_b, xf, w_out_b, l, seq)
        xf = _mlp(xf, mlp_g, w_up_b, w_down_b, final_g, l, apply_final_norm=(l == depth - 1))
    return xf.reshape(batch, seq, d_model)
```

```python
import functools
import math

import jax
import jax.numpy as jnp
from jax import lax
from jax.experimental import pallas as pl
from jax.experimental.pallas import tpu as pltpu

D_MODEL = 2048
GLA_HEADS = 4
GLA_DK = 128
GLA_DV = 256
GLA_GATE_RANK = 16
GLA_TAU = 16.0
GLA_CHUNK = 64
RET_HEADS = 4
RET_DK = 256
RET_DV = 256
ROPE_BASE = 10000.0
D_FF = 4 * D_MODEL
EPS = 1e-6

GLA_QK = GLA_HEADS * GLA_DK
GLA_V = GLA_HEADS * GLA_DV
RET_QK = RET_HEADS * RET_DK
RET_V = RET_HEADS * RET_DV
D_PROJ = 2 * GLA_QK + 2 * GLA_V + 2 * RET_QK + 2 * RET_V

OFF_GQ = 0
OFF_GK = OFF_GQ + GLA_QK
OFF_GV = OFF_GK + GLA_QK
OFF_GG = OFF_GV + GLA_V
OFF_RQ = OFF_GG + GLA_V
OFF_RK = OFF_RQ + RET_QK
OFF_RV = OFF_RK + RET_QK
OFF_RG = OFF_RV + RET_V

LANES = 128
MIX_BLOCK = 256
VMEM_LIMIT = 60 * 1024 * 1024

F32 = jnp.float32
BF16 = jnp.bfloat16

_NT = (((1,), (1,)), ((), ()))
_TN = (((0,), (0,)), ((), ()))


def _params(n_axes):
    return pltpu.CompilerParams(
        dimension_semantics=("arbitrary",) * n_axes, vmem_limit_bytes=VMEM_LIMIT)


def _rms(x, gain):
    return x * lax.rsqrt(jnp.mean(x * x, axis=-1, keepdims=True) + EPS) * gain


def _rope_kernel(pos_ref, invf_ref, cos_ref, sin_ref):
    rows = pos_ref.shape[0]
    invf = invf_ref[...]
    for r in range(rows):
        p = pos_ref[r:r + 1, :].astype(F32)
        col = jnp.broadcast_to(p, (LANES, LANES)).T
        ang = col * invf
        cos_ref[r * LANES:(r + 1) * LANES, :] = jnp.cos(ang)
        sin_ref[r * LANES:(r + 1) * LANES, :] = jnp.sin(ang)


def _rope_tables(positions):
    n_tok = positions.size
    rows = n_tok // LANES
    rb = 8
    pos2d = positions.reshape(rows, LANES)
    inv_freq = (ROPE_BASE ** (-jnp.arange(0, RET_DK, 2, dtype=F32) / RET_DK)).reshape(1, RET_DK // 2)
    out = jax.ShapeDtypeStruct((n_tok, RET_DK // 2), F32)
    return pl.pallas_call(
        _rope_kernel,
        grid=(rows // rb,),
        in_specs=[pl.BlockSpec((rb, LANES), lambda i: (i, 0)),
                  pl.BlockSpec((1, RET_DK // 2), lambda i: (0, 0))],
        out_specs=[pl.BlockSpec((rb * LANES, RET_DK // 2), lambda i: (i, 0))] * 2,
        out_shape=[out, out],
        compiler_params=_params(1),
        name="rope_tables",
    )(pos2d, inv_freq)


GATE_LO = 2 * GLA_QK + 2 * GLA_V
GATE_HI = GATE_LO + GLA_GATE_RANK


def _inproj_kernel(x_ref, g_ref, w_ref, wgr_ref, gup_ref, gb_ref, cos_ref, sin_ref,
                   proj_ref, loga_ref, h_ref, *, q_tile, k_tile):
    j = pl.program_id(1)

    @pl.when(j == 0)
    def _():
        hb = _rms(x_ref[...], g_ref[...]).astype(BF16)
        h_ref[...] = hb
        gr = lax.dot_general(hb, wgr_ref[...], _NT, preferred_element_type=F32)
        z = jnp.dot(gr.astype(BF16), gup_ref[...], preferred_element_type=F32) + gb_ref[...]
        log_sig = jnp.minimum(z, 0.0) - jnp.log1p(jnp.exp(-jnp.abs(z)))
        loga_ref[...] = log_sig / GLA_TAU

    is_rotary = (j == q_tile) | (j == k_tile)

    @pl.when(jnp.logical_not(is_rotary))
    def _():
        proj_ref[...] = lax.dot_general(h_ref[...], w_ref[...], _NT,
                                        preferred_element_type=F32).astype(proj_ref.dtype)

    @pl.when(is_rotary)
    def _():
        p = lax.dot_general(h_ref[...], w_ref[...], _NT, preferred_element_type=F32)
        scale = jnp.where(j == k_tile, RET_DK ** -0.5, 1.0).astype(F32)
        cos = cos_ref[...] * scale
        sin = sin_ref[...] * scale
        half = RET_DK // 2
        for h in range(RET_HEADS):
            lo = slice(h * RET_DK, h * RET_DK + half)
            hi = slice(h * RET_DK + half, (h + 1) * RET_DK)
            t1, t2 = p[:, lo], p[:, hi]
            proj_ref[:, lo] = (t1 * cos - t2 * sin).astype(proj_ref.dtype)
            proj_ref[:, hi] = (t2 * cos + t1 * sin).astype(proj_ref.dtype)


def _inproj(x, gain, w_all_t, w_gr_t, gate_up, gate_bias, cos, sin, layer, depth, tm=1024):
    n_tok = x.shape[0]
    tn = RET_QK
    assert OFF_RQ % tn == 0 and OFF_RK % tn == 0 and GATE_LO % tn == 0
    layer_rows = w_all_t.shape[0] // depth

    def w_rows(i, j):
        start = j * tn
        row = layer * layer_rows + jnp.where(start < GATE_LO, start, start + GLA_GATE_RANK)
        return pl.multiple_of(row, GLA_GATE_RANK), 0

    return pl.pallas_call(
        functools.partial(_inproj_kernel, q_tile=OFF_RQ // tn, k_tile=OFF_RK // tn),
        grid=(n_tok // tm, D_PROJ // tn),
        in_specs=[pl.BlockSpec((tm, D_MODEL), lambda i, j: (i, 0)),
                  pl.BlockSpec((None, 1, D_MODEL), lambda i, j: (layer, 0, 0)),
                  pl.BlockSpec((pl.Element(tn), pl.Element(D_MODEL)), w_rows),
                  pl.BlockSpec((None, LANES, D_MODEL), lambda i, j: (layer, 0, 0)),
                  pl.BlockSpec((None, LANES, GLA_QK), lambda i, j: (layer, 0, 0)),
                  pl.BlockSpec((None, 1, GLA_QK), lambda i, j: (layer, 0, 0)),
                  pl.BlockSpec((tm, RET_DK // 2), lambda i, j: (i, 0)),
                  pl.BlockSpec((tm, RET_DK // 2), lambda i, j: (i, 0))],
        out_specs=[pl.BlockSpec((tm, tn), lambda i, j: (i, j)),
                   pl.BlockSpec((tm, GLA_QK), lambda i, j: (i, 0))],
        out_shape=[jax.ShapeDtypeStruct((n_tok, D_PROJ), BF16),
                   jax.ShapeDtypeStruct((n_tok, GLA_QK), F32)],
        scratch_shapes=[pltpu.VMEM((tm, D_MODEL), BF16)],
        compiler_params=_params(2),
        name="norm_inproj",
    )(x, gain, w_all_t, w_gr_t, gate_up, gate_bias, cos, sin)


def _split3(x):
    hi = x.astype(BF16)
    r1 = x - hi.astype(F32)
    mid = r1.astype(BF16)
    lo = (r1 - mid.astype(F32)).astype(BF16)
    return hi, mid, lo


def _mix_block(proj_ref, loga_ref, gno_ref, rgain_ref, rbias_ref, mix_ref, sg_ref, sr_ref):
    t_blk = MIX_BLOCK
    n_chunk = t_blk // GLA_CHUNK

    row = lax.broadcasted_iota(jnp.int32, (t_blk, t_blk), 0)
    col = lax.broadcasted_iota(jnp.int32, (t_blk, t_blk), 1)
    causal = col <= row
    chunk_shift = int(math.log2(GLA_CHUNK))
    gla_mask = causal & ((row >> chunk_shift) == (col >> chunk_shift))
    rel = jnp.maximum(row - col, 0).astype(F32)
    rowf = lax.broadcasted_iota(jnp.int32, (t_blk, RET_DV), 0).astype(F32)

    cum = gla_mask.astype(BF16)
    la_hi, la_mid, la_lo = _split3(loga_ref[...])
    b_all = (jnp.dot(cum, la_hi, preferred_element_type=F32)
             + jnp.dot(cum, la_mid, preferred_element_type=F32)
             + jnp.dot(cum, la_lo, preferred_element_type=F32))

    for h in range(GLA_HEADS):
        q = proj_ref[:, OFF_GQ + h * GLA_DK:OFF_GQ + (h + 1) * GLA_DK].astype(F32)
        k = proj_ref[:, OFF_GK + h * GLA_DK:OFF_GK + (h + 1) * GLA_DK].astype(F32)
        v = proj_ref[:, OFF_GV + h * GLA_DV:OFF_GV + (h + 1) * GLA_DV]
        g = proj_ref[:, OFF_GG + h * GLA_DV:OFF_GG + (h + 1) * GLA_DV].astype(F32)
        b = b_all[:, h * GLA_DK:(h + 1) * GLA_DK]
        b_last = [b[(c + 1) * GLA_CHUNK - 1:(c + 1) * GLA_CHUNK, :] for c in range(n_chunk)]
        b_last_rows = jnp.concatenate(
            [jnp.broadcast_to(bl, (GLA_CHUNK, GLA_DK)) for bl in b_last], axis=0)
        q_dec = (q * (GLA_DK ** -0.5) * jnp.exp(b)).astype(BF16)
        k_dec = (k * jnp.exp(-b)).astype(BF16)
        k_tail = (k * jnp.exp(b_last_rows - b)).astype(BF16)

        s = lax.dot_general(q_dec, k_dec, _NT, preferred_element_type=F32)
        s = jnp.where(gla_mask, s, 0.0).astype(BF16)
        o_intra = jnp.dot(s, v, preferred_element_type=F32)

        state = sg_ref[h]
        o_parts = []
        for c in range(n_chunk):
            sl = slice(c * GLA_CHUNK, (c + 1) * GLA_CHUNK)
            o_parts.append(o_intra[sl] + jnp.dot(q_dec[sl], state.astype(BF16),
                                                 preferred_element_type=F32))
            local = lax.dot_general(k_tail[sl], v[sl], _TN, preferred_element_type=F32)
            dcol = jnp.broadcast_to(jnp.exp(b_last[c]), (GLA_DK, GLA_DK)).T
            state = jnp.concatenate([dcol, dcol], axis=1) * state + local
        sg_ref[h] = state

        o = jnp.concatenate(o_parts, axis=0)
        y = o * lax.rsqrt(jnp.mean(o * o, axis=-1, keepdims=True) + EPS)
        y = y * gno_ref[:, h * GLA_DV:(h + 1) * GLA_DV]
        y = y * (g / (1.0 + jnp.exp(-g)))
        mix_ref[:, h * GLA_DV:(h + 1) * GLA_DV] = y.astype(mix_ref.dtype)

    for h in range(RET_HEADS):
        log_gamma = math.log1p(-(2.0 ** (-5.0 - h)))
        qb = proj_ref[:, OFF_RQ + h * RET_DK:OFF_RQ + (h + 1) * RET_DK]
        kb = proj_ref[:, OFF_RK + h * RET_DK:OFF_RK + (h + 1) * RET_DK]
        v = proj_ref[:, OFF_RV + h * RET_DV:OFF_RV + (h + 1) * RET_DV]
        g = proj_ref[:, OFF_RG + h * RET_DV:OFF_RG + (h + 1) * RET_DV].astype(F32)
        decay = jnp.where(causal, jnp.exp(log_gamma * rel), 0.0)
        s = lax.dot_general(qb, kb, _NT, preferred_element_type=F32) * decay
        o_intra = jnp.dot(s.astype(BF16), v, preferred_element_type=F32)
        k_tail = (kb.astype(F32) * jnp.exp(log_gamma * (t_blk - 1.0 - rowf))).astype(BF16)
        local = lax.dot_general(k_tail, v, _TN, preferred_element_type=F32)
        state = sr_ref[h]
        o_inter = jnp.dot(qb, state.astype(BF16), preferred_element_type=F32)
        o = o_intra + o_inter * jnp.exp(log_gamma * (rowf + 1.0))
        sr_ref[h] = math.exp(log_gamma * t_blk) * state + local

        mu = jnp.mean(o, axis=-1, keepdims=True)
        cen = o - mu
        y = cen * lax.rsqrt(jnp.mean(cen * cen, axis=-1, keepdims=True) + EPS)
        y = y * rgain_ref[:, h * RET_DV:(h + 1) * RET_DV] + rbias_ref[:, h * RET_DV:(h + 1) * RET_DV]
        y = y * (g / (1.0 + jnp.exp(-g)))
        mix_ref[:, GLA_V + h * RET_DV:GLA_V + (h + 1) * RET_DV] = y.astype(mix_ref.dtype)


def _mixer_outproj_kernel(proj_ref, loga_ref, gno_ref, rgain_ref, rbias_ref, x_ref, wout_ref,
                          out_ref, sg_ref, sr_ref, mix_a, mix_b, *, blocks_per_seq):
    t = pl.program_id(0)

    @pl.when(lax.rem(t, blocks_per_seq) == 0)
    def _():
        sg_ref[...] = jnp.zeros_like(sg_ref)
        sr_ref[...] = jnp.zeros_like(sr_ref)

    @pl.when(t == 0)
    def _():
        mix_b[...] = jnp.zeros_like(mix_b)

    def step(src, dst):
        _mix_block(proj_ref, loga_ref, gno_ref, rgain_ref, rbias_ref, dst, sg_ref, sr_ref)
        out_ref[...] = x_ref[...] + jnp.dot(src[...], wout_ref[...], preferred_element_type=F32)

    parity = lax.rem(t, 2)

    @pl.when(parity == 0)
    def _():
        step(mix_b, mix_a)

    @pl.when(parity == 1)
    def _():
        step(mix_a, mix_b)


def _mixers_outproj(proj, loga, gla_norm, ret_gain, ret_bias, x, w_out, layer, seq):
    n_tok = proj.shape[0]
    t_blk = MIX_BLOCK
    n_blk = n_tok // t_blk
    cur = lambda t: (jnp.minimum(t, n_blk - 1), 0)
    prev = lambda t: (jnp.maximum(t - 1, 0), 0)
    const = lambda t: (layer, 0, 0)
    return pl.pallas_call(
        functools.partial(_mixer_outproj_kernel, blocks_per_seq=seq // t_blk),
        grid=(n_blk + 1,),
        in_specs=[pl.BlockSpec((t_blk, D_PROJ), cur),
                  pl.BlockSpec((t_blk, GLA_QK), cur),
                  pl.BlockSpec((None, 1, GLA_V), const),
                  pl.BlockSpec((None, 1, RET_V), const),
                  pl.BlockSpec((None, 1, RET_V), const),
                  pl.BlockSpec((t_blk, D_MODEL), prev),
                  pl.BlockSpec((None, D_MODEL, D_MODEL), const)],
        out_specs=pl.BlockSpec((t_blk, D_MODEL), prev),
        out_shape=jax.ShapeDtypeStruct((n_tok, D_MODEL), F32),
        scratch_shapes=[pltpu.VMEM((GLA_HEADS, GLA_DK, GLA_DV), F32),
                        pltpu.VMEM((RET_HEADS, RET_DK, RET_DV), F32),
                        pltpu.VMEM((t_blk, D_MODEL), BF16),
                        pltpu.VMEM((t_blk, D_MODEL), BF16)],
        compiler_params=_params(1),
        name="mixers_outproj",
    )(proj, loga, gla_norm, ret_gain, ret_bias, x, w_out)


def _mlp_kernel(x_ref, g_ref, wu_ref, wd_ref, fg_ref, o_ref, h_ref, *, apply_final_norm):
    k = pl.program_id(1)

    @pl.when(k == 0)
    def _():
        x = x_ref[...]
        h_ref[...] = _rms(x, g_ref[...]).astype(BF16)
        o_ref[...] = x

    u = jnp.dot(h_ref[...], wu_ref[...], preferred_element_type=F32)
    a = jnp.square(jnp.maximum(u, 0.0)).astype(BF16)
    o_ref[...] += jnp.dot(a, wd_ref[...], preferred_element_type=F32)

    if apply_final_norm:
        @pl.when(k == pl.num_programs(1) - 1)
        def _():
            o_ref[...] = _rms(o_ref[...], fg_ref[...])


def _mlp(x, gain, w_up, w_down, final_gain, layer, apply_final_norm, tm=1024, tf=1024):
    n_tok = x.shape[0]
    return pl.pallas_call(
        functools.partial(_mlp_kernel, apply_final_norm=apply_final_norm),
        grid=(n_tok // tm, D_FF // tf),
        in_specs=[pl.BlockSpec((tm, D_MODEL), lambda i, k: (i, 0)),
                  pl.BlockSpec((None, 1, D_MODEL), lambda i, k: (layer, 0, 0)),
                  pl.BlockSpec((None, D_MODEL, tf), lambda i, k: (layer, 0, k)),
                  pl.BlockSpec((None, tf, D_MODEL), lambda i, k: (layer, k, 0)),
                  pl.BlockSpec((1, D_MODEL), lambda i, k: (0, 0))],
        out_specs=pl.BlockSpec((tm, D_MODEL), lambda i, k: (i, 0)),
        out_shape=jax.ShapeDtypeStruct((n_tok, D_MODEL), F32),
        scratch_shapes=[pltpu.VMEM((tm, D_MODEL), BF16)],
        compiler_params=_params(2),
        name="norm_mlp_residual",
    )(x, gain, w_up, w_down, final_gain)


def kernel(x, positions, attn_norm, w_in, gla_gate_up, gla_gate_bias, gla_out_norm,
           ret_norm_gain, ret_norm_bias, w_out, mlp_norm, w_up, w_down, final_norm):
    batch, seq, d_model = x.shape
    depth = w_in.shape[0]
    assert d_model == D_MODEL and seq % MIX_BLOCK == 0
    n_tok = batch * seq
    xf = x.reshape(n_tok, d_model)

    cos, sin = _rope_tables(positions)

    pad = LANES - GLA_GATE_RANK
    w_in_t = jnp.swapaxes(w_in, 1, 2)
    w_in_tb = w_in_t.astype(BF16)
    w_gr_t = jnp.pad(w_in_tb[:, GATE_LO:GATE_HI], ((0, 0), (0, pad), (0, 0)))
    w_all_t = w_in_tb.reshape(depth * w_in.shape[2], d_model)
    gate_up = jnp.pad(gla_gate_up, ((0, 0), (0, pad), (0, 0))).astype(BF16)
    w_out_b = w_out.astype(BF16)
    w_up_b = w_up.astype(BF16)
    w_down_b = w_down.astype(BF16)
    row3 = lambda a: a.reshape(depth, 1, -1)
    attn_g, gate_b, mlp_g = row3(attn_norm), row3(gla_gate_bias), row3(mlp_norm)
    gla_g, ret_g, ret_b = row3(gla_out_norm), row3(ret_norm_gain), row3(ret_norm_bias)
    final_g = final_norm.reshape(1, -1)

    for l in range(depth):
        proj, loga = _inproj(xf, attn_g, w_all_t, w_gr_t, gate_up, gate_b, cos, sin, l, depth)
        xf = _mixers_outproj(proj, loga, gla_g, ret_g, ret_b, xf, w_out_b, l, seq)
        xf = _mlp(xf, mlp_g, w_up_b, w_down_b, final_g, l, apply_final_norm=(l == depth - 1))
    return xf.reshape(batch, seq, d_model)
```

```python
import functools
import math

import jax
import jax.numpy as jnp
from jax import lax
from jax.experimental import pallas as pl
from jax.experimental.pallas import tpu as pltpu

D_MODEL = 2048
GLA_HEADS = 4
GLA_DK = 128
GLA_DV = 256
GLA_GATE_RANK = 16
GLA_TAU = 16.0
GLA_CHUNK = 64
RET_HEADS = 4
RET_DK = 256
RET_DV = 256
ROPE_BASE = 10000.0
D_FF = 4 * D_MODEL
EPS = 1e-6

GLA_QK = GLA_HEADS * GLA_DK
GLA_V = GLA_HEADS * GLA_DV
RET_QK = RET_HEADS * RET_DK
RET_V = RET_HEADS * RET_DV
D_PROJ = 2 * GLA_QK + 2 * GLA_V + 2 * RET_QK + 2 * RET_V

OFF_GQ = 0
OFF_GK = OFF_GQ + GLA_QK
OFF_GV = OFF_GK + GLA_QK
OFF_GG = OFF_GV + GLA_V
OFF_RQ = OFF_GG + GLA_V
OFF_RK = OFF_RQ + RET_QK
OFF_RV = OFF_RK + RET_QK
OFF_RG = OFF_RV + RET_V

LANES = 128
MIX_BLOCK = 256
VMEM_LIMIT = 60 * 1024 * 1024

F32 = jnp.float32
BF16 = jnp.bfloat16

_NT = (((1,), (1,)), ((), ()))
_TN = (((0,), (0,)), ((), ()))


def _params(n_axes):
    return pltpu.CompilerParams(
        dimension_semantics=("arbitrary",) * n_axes, vmem_limit_bytes=VMEM_LIMIT)


def _rms(x, gain):
    return x * lax.rsqrt(jnp.mean(x * x, axis=-1, keepdims=True) + EPS) * gain


def _rope_kernel(pos_ref, invf_ref, cos_ref, sin_ref):
    rows = pos_ref.shape[0]
    invf = invf_ref[...]
    for r in range(rows):
        p = pos_ref[r:r + 1, :].astype(F32)
        col = jnp.broadcast_to(p, (LANES, LANES)).T
        ang = col * invf
        cos_ref[r * LANES:(r + 1) * LANES, :] = jnp.cos(ang)
        sin_ref[r * LANES:(r + 1) * LANES, :] = jnp.sin(ang)


def _rope_tables(positions):
    n_tok = positions.size
    rows = n_tok // LANES
    rb = 8
    pos2d = positions.reshape(rows, LANES)
    inv_freq = (ROPE_BASE ** (-jnp.arange(0, RET_DK, 2, dtype=F32) / RET_DK)).reshape(1, RET_DK // 2)
    out = jax.ShapeDtypeStruct((n_tok, RET_DK // 2), F32)
    return pl.pallas_call(
        _rope_kernel,
        grid=(rows // rb,),
        in_specs=[pl.BlockSpec((rb, LANES), lambda i: (i, 0)),
                  pl.BlockSpec((1, RET_DK // 2), lambda i: (0, 0))],
        out_specs=[pl.BlockSpec((rb * LANES, RET_DK // 2), lambda i: (i, 0))] * 2,
        out_shape=[out, out],
        compiler_params=_params(1),
        name="rope_tables",
    )(pos2d, inv_freq)


GATE_LO = 2 * GLA_QK + 2 * GLA_V
GATE_HI = GATE_LO + GLA_GATE_RANK


def _inproj_kernel(x_ref, g_ref, w_ref, wgr_ref, gup_ref, gb_ref, cos_ref, sin_ref,
                   proj_ref, loga_ref, h_ref, *, q_tile, k_tile):
    j = pl.program_id(1)

    @pl.when(j == 0)
    def _():
        hb = _rms(x_ref[...], g_ref[...]).astype(BF16)
        h_ref[...] = hb
        gr = lax.dot_general(hb, wgr_ref[...], _NT, preferred_element_type=F32)
        z = jnp.dot(gr.astype(BF16), gup_ref[...], preferred_element_type=F32) + gb_ref[...]
        log_sig = jnp.minimum(z, 0.0) - jnp.log(1.0 + jnp.exp(-jnp.abs(z)))
        loga_ref[...] = log_sig * (1.0 / GLA_TAU)

    is_rotary = (j == q_tile) | (j == k_tile)

    @pl.when(jnp.logical_not(is_rotary))
    def _():
        proj_ref[...] = lax.dot_general(h_ref[...], w_ref[...], _NT,
                                        preferred_element_type=F32).astype(proj_ref.dtype)

    @pl.when(is_rotary)
    def _():
        p = lax.dot_general(h_ref[...], w_ref[...], _NT, preferred_element_type=F32)
        scale = jnp.where(j == k_tile, RET_DK ** -0.5, 1.0).astype(F32)
        cos = cos_ref[...] * scale
        sin = sin_ref[...] * scale
        half = RET_DK // 2
        for h in range(RET_HEADS):
            lo = slice(h * RET_DK, h * RET_DK + half)
            hi = slice(h * RET_DK + half, (h + 1) * RET_DK)
            t1, t2 = p[:, lo], p[:, hi]
            proj_ref[:, lo] = (t1 * cos - t2 * sin).astype(proj_ref.dtype)
            proj_ref[:, hi] = (t2 * cos + t1 * sin).astype(proj_ref.dtype)


def _inproj(x, gain, w_all_t, w_gr_t, gate_up, gate_bias, cos, sin, layer, depth, tm=1024):
    n_tok = x.shape[0]
    tn = RET_QK
    assert OFF_RQ % tn == 0 and OFF_RK % tn == 0 and GATE_LO % tn == 0
    layer_rows = w_all_t.shape[0] // depth

    def w_rows(i, j):
        start = j * tn
        row = layer * layer_rows + jnp.where(start < GATE_LO, start, start + GLA_GATE_RANK)
        return pl.multiple_of(row, GLA_GATE_RANK), 0

    return pl.pallas_call(
        functools.partial(_inproj_kernel, q_tile=OFF_RQ // tn, k_tile=OFF_RK // tn),
        grid=(n_tok // tm, D_PROJ // tn),
        in_specs=[pl.BlockSpec((tm, D_MODEL), lambda i, j: (i, 0)),
                  pl.BlockSpec((None, 1, D_MODEL), lambda i, j: (layer, 0, 0)),
                  pl.BlockSpec((pl.Element(tn), pl.Element(D_MODEL)), w_rows),
                  pl.BlockSpec((None, LANES, D_MODEL), lambda i, j: (layer, 0, 0)),
                  pl.BlockSpec((None, LANES, GLA_QK), lambda i, j: (layer, 0, 0)),
                  pl.BlockSpec((None, 1, GLA_QK), lambda i, j: (layer, 0, 0)),
                  pl.BlockSpec((tm, RET_DK // 2), lambda i, j: (i, 0)),
                  pl.BlockSpec((tm, RET_DK // 2), lambda i, j: (i, 0))],
        out_specs=[pl.BlockSpec((tm, tn), lambda i, j: (i, j)),
                   pl.BlockSpec((tm, GLA_QK), lambda i, j: (i, 0))],
        out_shape=[jax.ShapeDtypeStruct((n_tok, D_PROJ), BF16),
                   jax.ShapeDtypeStruct((n_tok, GLA_QK), F32)],
        scratch_shapes=[pltpu.VMEM((tm, D_MODEL), BF16)],
        compiler_params=_params(2),
        name="norm_inproj",
    )(x, gain, w_all_t, w_gr_t, gate_up, gate_bias, cos, sin)


def _split3(x):
    hi = x.astype(BF16)
    r1 = x - hi.astype(F32)
    mid = r1.astype(BF16)
    lo = (r1 - mid.astype(F32)).astype(BF16)
    return hi, mid, lo


def _mix_block(proj_ref, loga_ref, gno_ref, rgain_ref, rbias_ref, mix_ref, sg_ref, sr_ref):
    t_blk = MIX_BLOCK
    n_chunk = t_blk // GLA_CHUNK

    row = lax.broadcasted_iota(jnp.int32, (t_blk, t_blk), 0)
    col = lax.broadcasted_iota(jnp.int32, (t_blk, t_blk), 1)
    causal = col <= row
    chunk_shift = int(math.log2(GLA_CHUNK))
    gla_mask = causal & ((row >> chunk_shift) == (col >> chunk_shift))
    rel = jnp.maximum(row - col, 0).astype(F32)
    rowf = lax.broadcasted_iota(jnp.int32, (t_blk, RET_DV), 0).astype(F32)

    cum = gla_mask.astype(BF16)
    la_hi, la_mid, la_lo = _split3(loga_ref[...])
    b_all = (jnp.dot(cum, la_hi, preferred_element_type=F32)
             + jnp.dot(cum, la_mid, preferred_element_type=F32)
             + jnp.dot(cum, la_lo, preferred_element_type=F32))

    for h in range(GLA_HEADS):
        q = proj_ref[:, OFF_GQ + h * GLA_DK:OFF_GQ + (h + 1) * GLA_DK].astype(F32)
        k = proj_ref[:, OFF_GK + h * GLA_DK:OFF_GK + (h + 1) * GLA_DK].astype(F32)
        v = proj_ref[:, OFF_GV + h * GLA_DV:OFF_GV + (h + 1) * GLA_DV]
        g = proj_ref[:, OFF_GG + h * GLA_DV:OFF_GG + (h + 1) * GLA_DV].astype(F32)
        b = b_all[:, h * GLA_DK:(h + 1) * GLA_DK]
        b_last = [b[(c + 1) * GLA_CHUNK - 1:(c + 1) * GLA_CHUNK, :] for c in range(n_chunk)]
        b_last_rows = jnp.concatenate(
            [jnp.broadcast_to(bl, (GLA_CHUNK, GLA_DK)) for bl in b_last], axis=0)
        q_dec = (q * (GLA_DK ** -0.5) * jnp.exp(b)).astype(BF16)
        k_dec = (k * jnp.exp(-b)).astype(BF16)
        k_tail = (k * jnp.exp(b_last_rows - b)).astype(BF16)

        s = lax.dot_general(q_dec, k_dec, _NT, preferred_element_type=F32)
        s = jnp.where(gla_mask, s, 0.0).astype(BF16)
        o_intra = jnp.dot(s, v, preferred_element_type=F32)

        state = sg_ref[h]
        o_parts = []
        for c in range(n_chunk):
            sl = slice(c * GLA_CHUNK, (c + 1) * GLA_CHUNK)
            o_parts.append(o_intra[sl] + jnp.dot(q_dec[sl], state.astype(BF16),
                                                 preferred_element_type=F32))
            local = lax.dot_general(k_tail[sl], v[sl], _TN, preferred_element_type=F32)
            dcol = jnp.broadcast_to(jnp.exp(b_last[c]), (GLA_DK, GLA_DK)).T
            state = jnp.concatenate([dcol, dcol], axis=1) * state + local
        sg_ref[h] = state

        o = jnp.concatenate(o_parts, axis=0)
        y = o * lax.rsqrt(jnp.mean(o * o, axis=-1, keepdims=True) + EPS)
        y = y * gno_ref[:, h * GLA_DV:(h + 1) * GLA_DV]
        y = y * (g / (1.0 + jnp.exp(-g)))
        mix_ref[:, h * GLA_DV:(h + 1) * GLA_DV] = y.astype(mix_ref.dtype)

    for h in range(RET_HEADS):
        log_gamma = math.log1p(-(2.0 ** (-5.0 - h)))
        qb = proj_ref[:, OFF_RQ + h * RET_DK:OFF_RQ + (h + 1) * RET_DK]
        kb = proj_ref[:, OFF_RK + h * RET_DK:OFF_RK + (h + 1) * RET_DK]
        v = proj_ref[:, OFF_RV + h * RET_DV:OFF_RV + (h + 1) * RET_DV]
        g = proj_ref[:, OFF_RG + h * RET_DV:OFF_RG + (h + 1) * RET_DV].astype(F32)
        decay = jnp.where(causal, jnp.exp(log_gamma * rel), 0.0)
        s = lax.dot_general(qb, kb, _NT, preferred_element_type=F32) * decay
        o_intra = jnp.dot(s.astype(BF16), v, preferred_element_type=F32)
        k_tail = (kb.astype(F32) * jnp.exp(log_gamma * (t_blk - 1.0 - rowf))).astype(BF16)
        local = lax.dot_general(k_tail, v, _TN, preferred_element_type=F32)
        state = sr_ref[h]
        o_inter = jnp.dot(qb, state.astype(BF16), preferred_element_type=F32)
        o = o_intra + o_inter * jnp.exp(log_gamma * (rowf + 1.0))
        sr_ref[h] = math.exp(log_gamma * t_blk) * state + local

        mu = jnp.mean(o, axis=-1, keepdims=True)
        cen = o - mu
        y = cen * lax.rsqrt(jnp.mean(cen * cen, axis=-1, keepdims=True) + EPS)
        y = y * rgain_ref[:, h * RET_DV:(h + 1) * RET_DV] + rbias_ref[:, h * RET_DV:(h + 1) * RET_DV]
        y = y * (g / (1.0 + jnp.exp(-g)))
        mix_ref[:, GLA_V + h * RET_DV:GLA_V + (h + 1) * RET_DV] = y.astype(mix_ref.dtype)


def _mixer_outproj_kernel(proj_ref, loga_ref, gno_ref, rgain_ref, rbias_ref, x_ref, wout_ref,
                          wup_ref, wdown_ref, out_ref, wup_b_ref, wdown_b_ref,
                          sg_ref, sr_ref, mix_a, mix_b, *, blocks_per_seq):
    t = pl.program_id(0)

    @pl.when(lax.rem(t, blocks_per_seq) == 0)
    def _():
        sg_ref[...] = jnp.zeros_like(sg_ref)
        sr_ref[...] = jnp.zeros_like(sr_ref)

    @pl.when(t == 0)
    def _():
        mix_b[...] = jnp.zeros_like(mix_b)

    def step(src, dst):
        wup_b_ref[...] = wup_ref[...].astype(wup_b_ref.dtype)
        wdown_b_ref[...] = wdown_ref[...].astype(wdown_b_ref.dtype)
        _mix_block(proj_ref, loga_ref, gno_ref, rgain_ref, rbias_ref, dst, sg_ref, sr_ref)
        out_ref[...] = x_ref[...] + jnp.dot(src[...], wout_ref[...], preferred_element_type=F32)

    parity = lax.rem(t, 2)

    @pl.when(parity == 0)
    def _():
        step(mix_b, mix_a)

    @pl.when(parity == 1)
    def _():
        step(mix_a, mix_b)


def _mixers_outproj(proj, loga, gla_norm, ret_gain, ret_bias, x, w_out, w_up, w_down, layer, seq):
    n_tok = proj.shape[0]
    t_blk = MIX_BLOCK
    n_blk = n_tok // t_blk
    cur = lambda t: (jnp.minimum(t, n_blk - 1), 0)
    prev = lambda t: (jnp.maximum(t - 1, 0), 0)
    const = lambda t: (layer, 0, 0)
    slab = lambda t: (layer, jnp.minimum(t, n_blk - 1), 0)
    up_rows, down_rows = D_MODEL // n_blk, D_FF // n_blk
    return pl.pallas_call(
        functools.partial(_mixer_outproj_kernel, blocks_per_seq=seq // t_blk),
        grid=(n_blk + 1,),
        in_specs=[pl.BlockSpec((t_blk, D_PROJ), cur),
                  pl.BlockSpec((t_blk, GLA_QK), cur),
                  pl.BlockSpec((None, 1, GLA_V), const),
                  pl.BlockSpec((None, 1, RET_V), const),
                  pl.BlockSpec((None, 1, RET_V), const),
                  pl.BlockSpec((t_blk, D_MODEL), prev),
                  pl.BlockSpec((None, D_MODEL, D_MODEL), const),
                  pl.BlockSpec((None, up_rows, D_FF), slab),
                  pl.BlockSpec((None, down_rows, D_MODEL), slab)],
        out_specs=[pl.BlockSpec((t_blk, D_MODEL), prev),
                   pl.BlockSpec((up_rows, D_FF), cur),
                   pl.BlockSpec((down_rows, D_MODEL), cur)],
        out_shape=[jax.ShapeDtypeStruct((n_tok, D_MODEL), F32),
                   jax.ShapeDtypeStruct((D_MODEL, D_FF), BF16),
                   jax.ShapeDtypeStruct((D_FF, D_MODEL), BF16)],
        scratch_shapes=[pltpu.VMEM((GLA_HEADS, GLA_DK, GLA_DV), F32),
                        pltpu.VMEM((RET_HEADS, RET_DK, RET_DV), F32),
                        pltpu.VMEM((t_blk, D_MODEL), BF16),
                        pltpu.VMEM((t_blk, D_MODEL), BF16)],
        compiler_params=_params(1),
        name="mixers_outproj",
    )(proj, loga, gla_norm, ret_gain, ret_bias, x, w_out, w_up, w_down)


def _mlp_kernel(x_ref, g_ref, wu_ref, wd_ref, fg_ref, o_ref, h_ref, *, apply_final_norm):
    k = pl.program_id(1)

    @pl.when(k == 0)
    def _():
        x = x_ref[...]
        h_ref[...] = _rms(x, g_ref[...]).astype(BF16)
        o_ref[...] = x

    u = jnp.dot(h_ref[...], wu_ref[...], preferred_element_type=F32)
    a = jnp.square(jnp.maximum(u, 0.0)).astype(BF16)
    o_ref[...] += jnp.dot(a, wd_ref[...], preferred_element_type=F32)

    if apply_final_norm:
        @pl.when(k == pl.num_programs(1) - 1)
        def _():
            o_ref[...] = _rms(o_ref[...], fg_ref[...])


def _mlp(x, gain, w_up, w_down, final_gain, layer, apply_final_norm, tm=1024, tf=1024):
    n_tok = x.shape[0]
    return pl.pallas_call(
        functools.partial(_mlp_kernel, apply_final_norm=apply_final_norm),
        grid=(n_tok // tm, D_FF // tf),
        in_specs=[pl.BlockSpec((tm, D_MODEL), lambda i, k: (i, 0)),
                  pl.BlockSpec((None, 1, D_MODEL), lambda i, k: (layer, 0, 0)),
                  pl.BlockSpec((D_MODEL, tf), lambda i, k: (0, k)),
                  pl.BlockSpec((tf, D_MODEL), lambda i, k: (k, 0)),
                  pl.BlockSpec((1, D_MODEL), lambda i, k: (0, 0))],
        out_specs=pl.BlockSpec((tm, D_MODEL), lambda i, k: (i, 0)),
        out_shape=jax.ShapeDtypeStruct((n_tok, D_MODEL), F32),
        scratch_shapes=[pltpu.VMEM((tm, D_MODEL), BF16)],
        compiler_params=_params(2),
        name="norm_mlp_residual",
    )(x, gain, w_up, w_down, final_gain)


def kernel(x, positions, attn_norm, w_in, gla_gate_up, gla_gate_bias, gla_out_norm,
           ret_norm_gain, ret_norm_bias, w_out, mlp_norm, w_up, w_down, final_norm):
    batch, seq, d_model = x.shape
    depth = w_in.shape[0]
    assert d_model == D_MODEL and seq % MIX_BLOCK == 0
    n_tok = batch * seq
    xf = x.reshape(n_tok, d_model)

    cos, sin = _rope_tables(positions)

    pad = LANES - GLA_GATE_RANK
    w_in_t = jnp.swapaxes(w_in, 1, 2)
    w_gr_t = jnp.pad(w_in[:, :, GATE_LO:GATE_HI].swapaxes(1, 2), ((0, 0), (0, pad), (0, 0))).astype(BF16)
    w_all_t = w_in_t.astype(BF16).reshape(depth * w_in.shape[2], d_model)
    gate_up = jnp.pad(gla_gate_up, ((0, 0), (0, pad), (0, 0))).astype(BF16)
    w_out_b = w_out.astype(BF16)
    row3 = lambda a: a.reshape(depth, 1, -1)
    attn_g, gate_b, mlp_g = row3(attn_norm), row3(gla_gate_bias), row3(mlp_norm)
    gla_g, ret_g, ret_b = row3(gla_out_norm), row3(ret_norm_gain), row3(ret_norm_bias)
    final_g = final_norm.reshape(1, -1)

    for l in range(depth):
        proj, loga = _inproj(xf, attn_g, w_all_t, w_gr_t, gate_up, gate_b, cos, sin, l, depth)
        xf, w_up_b, w_down_b = _mixers_outproj(proj, loga, gla_g, ret_g, ret_b, xf, w_out_b, w_up, w_down, l, seq)
        xf = _mlp(xf, mlp_g, w_up_b, w_down_b, final_g, l, apply_final_norm=(l == depth - 1))
    return xf.reshape(batch, seq, d_model)
```

```python
import functools
import math

import jax
import jax.numpy as jnp
import numpy as np
from jax import lax
from jax.experimental import pallas as pl
from jax.experimental.pallas import tpu as pltpu

D_MODEL = 2048
GLA_HEADS = 4
GLA_DK = 128
GLA_DV = 256
GLA_GATE_RANK = 16
GLA_TAU = 16.0
GLA_CHUNK = 64
RET_HEADS = 4
RET_DK = 256
RET_DV = 256
ROPE_BASE = 10000.0
D_FF = 4 * D_MODEL
EPS = 1e-6

GLA_QK = GLA_HEADS * GLA_DK
GLA_V = GLA_HEADS * GLA_DV
RET_QK = RET_HEADS * RET_DK
RET_V = RET_HEADS * RET_DV
D_PROJ = 2 * GLA_QK + 2 * GLA_V + 2 * RET_QK + 2 * RET_V

OFF_GQ = 0
OFF_GK = OFF_GQ + GLA_QK
OFF_GV = OFF_GK + GLA_QK
OFF_GG = OFF_GV + GLA_V
OFF_RQ = OFF_GG + GLA_V
OFF_RK = OFF_RQ + RET_QK
OFF_RV = OFF_RK + RET_QK
OFF_RG = OFF_RV + RET_V

LANES = 128
MIX_BLOCK = 256
VMEM_LIMIT = 60 * 1024 * 1024

F32 = jnp.float32
BF16 = jnp.bfloat16

_NT = (((1,), (1,)), ((), ()))
_TN = (((0,), (0,)), ((), ()))


def _params(n_axes):
    return pltpu.CompilerParams(
        dimension_semantics=("arbitrary",) * n_axes, vmem_limit_bytes=VMEM_LIMIT)


def _rms(x, gain):
    return x * lax.rsqrt(jnp.mean(x * x, axis=-1, keepdims=True) + EPS) * gain


def _rope_kernel(pos_ref, invf_ref, cos_ref, sin_ref):
    rows = pos_ref.shape[0]
    invf = invf_ref[...]
    for r in range(rows):
        p = pos_ref[r:r + 1, :].astype(F32)
        col = jnp.broadcast_to(p, (LANES, LANES)).T
        ang = col * invf
        cos_ref[r * LANES:(r + 1) * LANES, :] = jnp.cos(ang)
        sin_ref[r * LANES:(r + 1) * LANES, :] = jnp.sin(ang)


def _rope_tables(positions):
    n_tok = positions.size
    rows = n_tok // LANES
    rb = 8
    pos2d = positions.reshape(rows, LANES)
    inv_freq = (ROPE_BASE ** (-jnp.arange(0, RET_DK, 2, dtype=F32) / RET_DK)).reshape(1, RET_DK // 2)
    out = jax.ShapeDtypeStruct((n_tok, RET_DK // 2), F32)
    return pl.pallas_call(
        _rope_kernel,
        grid=(rows // rb,),
        in_specs=[pl.BlockSpec((rb, LANES), lambda i: (i, 0)),
                  pl.BlockSpec((1, RET_DK // 2), lambda i: (0, 0))],
        out_specs=[pl.BlockSpec((rb * LANES, RET_DK // 2), lambda i: (i, 0))] * 2,
        out_shape=[out, out],
        compiler_params=_params(1),
        name="rope_tables",
    )(pos2d, inv_freq)


GATE_LO = 2 * GLA_QK + 2 * GLA_V
GATE_HI = GATE_LO + GLA_GATE_RANK


def _inproj_kernel(x_ref, g_ref, w_ref, wgr_ref, gup_ref, gb_ref, cos_ref, sin_ref, wout_ref,
                   proj_ref, loga_ref, wout_b_ref, h_ref, *, q_tile, k_tile):
    j = pl.program_id(1)

    def project(h):
        return lax.dot_general(h, w_ref[...], _NT, preferred_element_type=F32)

    @pl.when(j == 0)
    def _():
        wout_b_ref[...] = wout_ref[...].astype(wout_b_ref.dtype)
        hb = _rms(x_ref[...], g_ref[...]).astype(BF16)
        h_ref[...] = hb
        gr = lax.dot_general(hb, wgr_ref[...], _NT, preferred_element_type=F32)
        z = jnp.dot(gr.astype(BF16), gup_ref[...], preferred_element_type=F32) + gb_ref[...]
        log_sig = jnp.minimum(z, 0.0) - jnp.log(1.0 + jnp.exp(-jnp.abs(z)))
        loga_ref[...] = log_sig * (1.0 / GLA_TAU)
        proj_ref[...] = project(hb).astype(proj_ref.dtype)

    is_rotary = (j == q_tile) | (j == k_tile)

    @pl.when((j != 0) & jnp.logical_not(is_rotary))
    def _():
        proj_ref[...] = project(h_ref[...]).astype(proj_ref.dtype)

    @pl.when(is_rotary)
    def _():
        p = project(h_ref[...])
        scale = jnp.where(j == k_tile, RET_DK ** -0.5, 1.0).astype(F32)
        cos = cos_ref[...] * scale
        sin = sin_ref[...] * scale
        half = RET_DK // 2
        for h in range(RET_HEADS):
            lo = slice(h * RET_DK, h * RET_DK + half)
            hi = slice(h * RET_DK + half, (h + 1) * RET_DK)
            t1, t2 = p[:, lo], p[:, hi]
            proj_ref[:, lo] = (t1 * cos - t2 * sin).astype(proj_ref.dtype)
            proj_ref[:, hi] = (t2 * cos + t1 * sin).astype(proj_ref.dtype)


def _inproj(x, gain, w_all_t, w_gr_t, gate_up, gate_bias, cos, sin, w_out, layer, depth, tm=1024):
    n_tok = x.shape[0]
    wout_rows = D_MODEL // (n_tok // tm)
    tn = RET_QK
    assert OFF_RQ % tn == 0 and OFF_RK % tn == 0 and GATE_LO % tn == 0 and OFF_RQ >= tn
    layer_rows = w_all_t.shape[0] // depth

    def w_rows(i, j):
        start = j * tn
        row = layer * layer_rows + jnp.where(start < GATE_LO, start, start + GLA_GATE_RANK)
        return pl.multiple_of(row, GLA_GATE_RANK), 0

    return pl.pallas_call(
        functools.partial(_inproj_kernel, q_tile=OFF_RQ // tn, k_tile=OFF_RK // tn),
        grid=(n_tok // tm, D_PROJ // tn),
        in_specs=[pl.BlockSpec((tm, D_MODEL), lambda i, j: (i, 0)),
                  pl.BlockSpec((None, 1, D_MODEL), lambda i, j: (layer, 0, 0)),
                  pl.BlockSpec((pl.Element(tn), pl.Element(D_MODEL)), w_rows),
                  pl.BlockSpec((None, LANES, D_MODEL), lambda i, j: (layer, 0, 0)),
                  pl.BlockSpec((None, LANES, GLA_QK), lambda i, j: (layer, 0, 0)),
                  pl.BlockSpec((None, 1, GLA_QK), lambda i, j: (layer, 0, 0)),
                  pl.BlockSpec((tm, RET_DK // 2), lambda i, j: (i, 0)),
                  pl.BlockSpec((tm, RET_DK // 2), lambda i, j: (i, 0)),
                  pl.BlockSpec((None, wout_rows, D_MODEL), lambda i, j: (layer, i, 0))],
        out_specs=[pl.BlockSpec((tm, tn), lambda i, j: (i, j)),
                   pl.BlockSpec((tm, GLA_QK), lambda i, j: (i, 0)),
                   pl.BlockSpec((wout_rows, D_MODEL), lambda i, j: (i, 0))],
        out_shape=[jax.ShapeDtypeStruct((n_tok, D_PROJ), BF16),
                   jax.ShapeDtypeStruct((n_tok, GLA_QK), F32),
                   jax.ShapeDtypeStruct((D_MODEL, D_MODEL), BF16)],
        scratch_shapes=[pltpu.VMEM((tm, D_MODEL), BF16)],
        compiler_params=_params(2),
        name="norm_inproj",
    )(x, gain, w_all_t, w_gr_t, gate_up, gate_bias, cos, sin, w_out)


def _split3(x):
    hi = x.astype(BF16)
    r1 = x - hi.astype(F32)
    mid = r1.astype(BF16)
    lo = (r1 - mid.astype(F32)).astype(BF16)
    return hi, mid, lo


def _ret_gamma(h):
    return 1.0 - 2.0 ** (-5.0 - h)


def _ret_decay_tables(t_blk):
    assert t_blk == RET_DK == RET_DV
    log_gamma = np.log1p(-(2.0 ** (-5.0 - np.arange(RET_HEADS))))[:, None, None]
    idx = np.arange(t_blk, dtype=np.float64)
    rel = idx[:, None] - idx[None, :]
    decay = np.where(rel >= 0.0, np.exp(log_gamma * np.maximum(rel, 0.0)), 0.0)
    tail = np.broadcast_to(np.exp(log_gamma * (t_blk - 1.0 - idx[None, :, None])), decay.shape)
    inner = np.broadcast_to(np.exp(log_gamma * (idx[None, :, None] + 1.0)), decay.shape)
    return jnp.asarray(np.stack([decay, tail, inner], axis=1), dtype=F32)


def _silu(g):
    half_g = 0.5 * g
    return half_g + half_g * jnp.tanh(half_g)


def _mix_block(proj_ref, loga_ref, gno_ref, rgain_ref, rbias_ref, dec_ref, mix_ref, sg_ref, sr_ref):
    t_blk = MIX_BLOCK
    n_chunk = t_blk // GLA_CHUNK

    row = lax.broadcasted_iota(jnp.int32, (t_blk, t_blk), 0)
    col = lax.broadcasted_iota(jnp.int32, (t_blk, t_blk), 1)
    causal = col <= row
    chunk_shift = int(math.log2(GLA_CHUNK))
    gla_mask = causal & ((row >> chunk_shift) == (col >> chunk_shift))

    cum = gla_mask.astype(BF16)
    la_hi, la_mid, la_lo = _split3(loga_ref[...])
    b_all = (jnp.dot(cum, la_hi, preferred_element_type=F32)
             + jnp.dot(cum, la_mid, preferred_element_type=F32)
             + jnp.dot(cum, la_lo, preferred_element_type=F32))

    for h in range(GLA_HEADS):
        q = proj_ref[:, OFF_GQ + h * GLA_DK:OFF_GQ + (h + 1) * GLA_DK].astype(F32)
        k = proj_ref[:, OFF_GK + h * GLA_DK:OFF_GK + (h + 1) * GLA_DK].astype(F32)
        v = proj_ref[:, OFF_GV + h * GLA_DV:OFF_GV + (h + 1) * GLA_DV]
        g = proj_ref[:, OFF_GG + h * GLA_DV:OFF_GG + (h + 1) * GLA_DV].astype(F32)
        b = b_all[:, h * GLA_DK:(h + 1) * GLA_DK]
        b_last = [b[(c + 1) * GLA_CHUNK - 1:(c + 1) * GLA_CHUNK, :] for c in range(n_chunk)]
        b_last_rows = jnp.concatenate(
            [jnp.broadcast_to(bl, (GLA_CHUNK, GLA_DK)) for bl in b_last], axis=0)
        q_dec = (q * (GLA_DK ** -0.5) * jnp.exp(b)).astype(BF16)
        k_dec = (k * jnp.exp(-b)).astype(BF16)
        k_tail = (k * jnp.exp(b_last_rows - b)).astype(BF16)

        s = lax.dot_general(q_dec, k_dec, _NT, preferred_element_type=F32)
        s = jnp.where(gla_mask, s, 0.0).astype(BF16)
        o_intra = jnp.dot(s, v, preferred_element_type=F32)

        state = sg_ref[h]
        o_parts = []
        for c in range(n_chunk):
            sl = slice(c * GLA_CHUNK, (c + 1) * GLA_CHUNK)
            o_parts.append(o_intra[sl] + jnp.dot(q_dec[sl], state.astype(BF16),
                                                 preferred_element_type=F32))
            local = lax.dot_general(k_tail[sl], v[sl], _TN, preferred_element_type=F32)
            dcol = jnp.broadcast_to(jnp.exp(b_last[c]), (GLA_DK, GLA_DK)).T
            state = jnp.concatenate([dcol, dcol], axis=1) * state + local
        sg_ref[h] = state

        o = jnp.concatenate(o_parts, axis=0)
        y = o * lax.rsqrt(jnp.mean(o * o, axis=-1, keepdims=True) + EPS)
        y = y * gno_ref[:, h * GLA_DV:(h + 1) * GLA_DV]
        y = y * _silu(g)
        mix_ref[:, h * GLA_DV:(h + 1) * GLA_DV] = y.astype(mix_ref.dtype)

    for h in range(RET_HEADS):
        qb = proj_ref[:, OFF_RQ + h * RET_DK:OFF_RQ + (h + 1) * RET_DK]
        kb = proj_ref[:, OFF_RK + h * RET_DK:OFF_RK + (h + 1) * RET_DK]
        v = proj_ref[:, OFF_RV + h * RET_DV:OFF_RV + (h + 1) * RET_DV]
        g = proj_ref[:, OFF_RG + h * RET_DV:OFF_RG + (h + 1) * RET_DV].astype(F32)
        s = lax.dot_general(qb, kb, _NT, preferred_element_type=F32) * dec_ref[h, 0]
        o_intra = jnp.dot(s.astype(BF16), v, preferred_element_type=F32)
        k_tail = (kb.astype(F32) * dec_ref[h, 1]).astype(BF16)
        local = lax.dot_general(k_tail, v, _TN, preferred_element_type=F32)
        state = sr_ref[h]
        o_inter = jnp.dot(qb, state.astype(BF16), preferred_element_type=F32)
        o = o_intra + o_inter * dec_ref[h, 2]
        sr_ref[h] = _ret_gamma(h) ** t_blk * state + local

        mu = jnp.mean(o, axis=-1, keepdims=True)
        cen = o - mu
        y = cen * lax.rsqrt(jnp.mean(cen * cen, axis=-1, keepdims=True) + EPS)
        y = y * rgain_ref[:, h * RET_DV:(h + 1) * RET_DV] + rbias_ref[:, h * RET_DV:(h + 1) * RET_DV]
        y = y * _silu(g)
        mix_ref[:, GLA_V + h * RET_DV:GLA_V + (h + 1) * RET_DV] = y.astype(mix_ref.dtype)


def _mixer_outproj_kernel(proj_ref, loga_ref, gno_ref, rgain_ref, rbias_ref, dec_ref, x_ref, wout_ref,
                          wup_ref, wdown_ref, out_ref, wup_b_ref, wdown_b_ref,
                          sg_ref, sr_ref, mix_a, mix_b, *, blocks_per_seq):
    t = pl.program_id(0)

    @pl.when(lax.rem(t, blocks_per_seq) == 0)
    def _():
        sg_ref[...] = jnp.zeros_like(sg_ref)
        sr_ref[...] = jnp.zeros_like(sr_ref)

    @pl.when(t == 0)
    def _():
        mix_b[...] = jnp.zeros_like(mix_b)

    def step(src, dst):
        wup_b_ref[...] = wup_ref[...].astype(wup_b_ref.dtype)
        wdown_b_ref[...] = wdown_ref[...].astype(wdown_b_ref.dtype)
        _mix_block(proj_ref, loga_ref, gno_ref, rgain_ref, rbias_ref, dec_ref, dst, sg_ref, sr_ref)
        out_ref[...] = x_ref[...] + jnp.dot(src[...], wout_ref[...], preferred_element_type=F32)

    parity = lax.rem(t, 2)

    @pl.when(parity == 0)
    def _():
        step(mix_b, mix_a)

    @pl.when(parity == 1)
    def _():
        step(mix_a, mix_b)


def _mixers_outproj(proj, loga, gla_norm, ret_gain, ret_bias, x, w_out, w_up, w_down, layer, seq):
    n_tok = proj.shape[0]
    t_blk = MIX_BLOCK
    n_blk = n_tok // t_blk
    cur = lambda t: (jnp.minimum(t, n_blk - 1), 0)
    prev = lambda t: (jnp.maximum(t - 1, 0), 0)
    const = lambda t: (layer, 0, 0)
    slab = lambda t: (layer, jnp.minimum(t, n_blk - 1), 0)
    up_rows, down_rows = D_MODEL // n_blk, D_FF // n_blk
    return pl.pallas_call(
        functools.partial(_mixer_outproj_kernel, blocks_per_seq=seq // t_blk),
        grid=(n_blk + 1,),
        in_specs=[pl.BlockSpec((t_blk, D_PROJ), cur),
                  pl.BlockSpec((t_blk, GLA_QK), cur),
                  pl.BlockSpec((None, 1, GLA_V), const),
                  pl.BlockSpec((None, 1, RET_V), const),
                  pl.BlockSpec((None, 1, RET_V), const),
                  pl.BlockSpec((RET_HEADS, 3, t_blk, t_blk), lambda t: (0, 0, 0, 0)),
                  pl.BlockSpec((t_blk, D_MODEL), prev),
                  pl.BlockSpec((D_MODEL, D_MODEL), lambda t: (0, 0)),
                  pl.BlockSpec((None, up_rows, D_FF), slab),
                  pl.BlockSpec((None, down_rows, D_MODEL), slab)],
        out_specs=[pl.BlockSpec((t_blk, D_MODEL), prev),
                   pl.BlockSpec((up_rows, D_FF), cur),
                   pl.BlockSpec((down_rows, D_MODEL), cur)],
        out_shape=[jax.ShapeDtypeStruct((n_tok, D_MODEL), F32),
                   jax.ShapeDtypeStruct((D_MODEL, D_FF), BF16),
                   jax.ShapeDtypeStruct((D_FF, D_MODEL), BF16)],
        scratch_shapes=[pltpu.VMEM((GLA_HEADS, GLA_DK, GLA_DV), F32),
                        pltpu.VMEM((RET_HEADS, RET_DK, RET_DV), F32),
                        pltpu.VMEM((t_blk, D_MODEL), BF16),
                        pltpu.VMEM((t_blk, D_MODEL), BF16)],
        compiler_params=_params(1),
        name="mixers_outproj",
    )(proj, loga, gla_norm, ret_gain, ret_bias, _ret_decay_tables(t_blk), x, w_out, w_up, w_down)


def _mlp_kernel(x_ref, g_ref, wu_ref, wd_ref, fg_ref, o_ref, h_ref, *, apply_final_norm):
    k = pl.program_id(1)

    def ff_tile(h):
        u = jnp.dot(h, wu_ref[...], preferred_element_type=F32)
        a = jnp.square(jnp.maximum(u, 0.0)).astype(BF16)
        return jnp.dot(a, wd_ref[...], preferred_element_type=F32)

    @pl.when(k == 0)
    def _():
        x = x_ref[...]
        h = _rms(x, g_ref[...]).astype(BF16)
        h_ref[...] = h
        o_ref[...] = x + ff_tile(h)

    @pl.when(k != 0)
    def _():
        o_ref[...] += ff_tile(h_ref[...])

    if apply_final_norm:
        @pl.when(k == pl.num_programs(1) - 1)
        def _():
            o_ref[...] = _rms(o_ref[...], fg_ref[...])


def _mlp(x, gain, w_up, w_down, final_gain, layer, apply_final_norm, tm=1024, tf=1024):
    n_tok = x.shape[0]
    return pl.pallas_call(
        functools.partial(_mlp_kernel, apply_final_norm=apply_final_norm),
        grid=(n_tok // tm, D_FF // tf),
        in_specs=[pl.BlockSpec((tm, D_MODEL), lambda i, k: (i, 0)),
                  pl.BlockSpec((None, 1, D_MODEL), lambda i, k: (layer, 0, 0)),
                  pl.BlockSpec((D_MODEL, tf), lambda i, k: (0, k)),
                  pl.BlockSpec((tf, D_MODEL), lambda i, k: (k, 0)),
                  pl.BlockSpec((1, D_MODEL), lambda i, k: (0, 0))],
        out_specs=pl.BlockSpec((tm, D_MODEL), lambda i, k: (i, 0)),
        out_shape=jax.ShapeDtypeStruct((n_tok, D_MODEL), F32),
        scratch_shapes=[pltpu.VMEM((tm, D_MODEL), BF16)],
        compiler_params=_params(2),
        name="norm_mlp_residual",
    )(x, gain, w_up, w_down, final_gain)


def kernel(x, positions, attn_norm, w_in, gla_gate_up, gla_gate_bias, gla_out_norm,
           ret_norm_gain, ret_norm_bias, w_out, mlp_norm, w_up, w_down, final_norm):
    batch, seq, d_model = x.shape
    depth = w_in.shape[0]
    assert d_model == D_MODEL and seq % MIX_BLOCK == 0
    n_tok = batch * seq
    xf = x.reshape(n_tok, d_model)

    cos, sin = _rope_tables(positions)

    pad = LANES - GLA_GATE_RANK
    w_in_t = jnp.swapaxes(w_in, 1, 2)
    w_gr_t = jnp.pad(w_in[:, :, GATE_LO:GATE_HI].swapaxes(1, 2), ((0, 0), (0, pad), (0, 0))).astype(BF16)
    w_all_t = w_in_t.astype(BF16).reshape(depth * w_in.shape[2], d_model)
    gate_up = jnp.pad(gla_gate_up, ((0, 0), (0, pad), (0, 0))).astype(BF16)
    row3 = lambda a: a.reshape(depth, 1, -1)
    attn_g, gate_b, mlp_g = row3(attn_norm), row3(gla_gate_bias), row3(mlp_norm)
    gla_g, ret_g, ret_b = row3(gla_out_norm), row3(ret_norm_gain), row3(ret_norm_bias)
    final_g = final_norm.reshape(1, -1)

    for l in range(depth):
        proj, loga, w_out_b = _inproj(xf, attn_g, w_all_t, w_gr_t, gate_up, gate_b, cos, sin, w_out, l, depth)
        xf, w_up_b, w_down_b = _mixers_outproj(proj, loga, gla_g, ret_g, ret_b, xf, w_out_b, w_up, w_down, l, seq)
        xf = _mlp(xf, mlp_g, w_up_b, w_down_b, final_g, l, apply_final_norm=(l == depth - 1))
    return xf.reshape(batch, seq, d_model)
```

```python
import functools
import math

import jax
import jax.numpy as jnp
import numpy as np
from jax import lax
from jax.experimental import pallas as pl
from jax.experimental.pallas import tpu as pltpu

D_MODEL = 2048
GLA_HEADS = 4
GLA_DK = 128
GLA_DV = 256
GLA_GATE_RANK = 16
GLA_TAU = 16.0
GLA_CHUNK = 64
RET_HEADS = 4
RET_DK = 256
RET_DV = 256
ROPE_BASE = 10000.0
D_FF = 4 * D_MODEL
EPS = 1e-6

GLA_QK = GLA_HEADS * GLA_DK
GLA_V = GLA_HEADS * GLA_DV
RET_QK = RET_HEADS * RET_DK
RET_V = RET_HEADS * RET_DV
D_PROJ = 2 * GLA_QK + 2 * GLA_V + 2 * RET_QK + 2 * RET_V

OFF_GQ = 0
OFF_GK = OFF_GQ + GLA_QK
OFF_GV = OFF_GK + GLA_QK
OFF_GG = OFF_GV + GLA_V
OFF_RQ = OFF_GG + GLA_V
OFF_RK = OFF_RQ + RET_QK
OFF_RV = OFF_RK + RET_QK
OFF_RG = OFF_RV + RET_V

LANES = 128
MIX_BLOCK = 256
MLP_FF_TILE = 1024
WIN_SLAB = 128
VMEM_LIMIT = 60 * 1024 * 1024

F32 = jnp.float32
BF16 = jnp.bfloat16

_NT = (((1,), (1,)), ((), ()))
_TN = (((0,), (0,)), ((), ()))


def _params(n_axes):
    return pltpu.CompilerParams(
        dimension_semantics=("arbitrary",) * n_axes, vmem_limit_bytes=VMEM_LIMIT)


def _rms(x, gain):
    return x * lax.rsqrt(jnp.mean(x * x, axis=-1, keepdims=True) + EPS) * gain


def _rope_kernel(pos_ref, invf_ref, cos_ref, sin_ref):
    rows = pos_ref.shape[0]
    invf = invf_ref[...]
    for r in range(rows):
        p = pos_ref[r:r + 1, :].astype(F32)
        col = jnp.broadcast_to(p, (LANES, LANES)).T
        ang = col * invf
        cos_ref[r * LANES:(r + 1) * LANES, :] = jnp.cos(ang)
        sin_ref[r * LANES:(r + 1) * LANES, :] = jnp.sin(ang)


def _rope_tables(positions):
    n_tok = positions.size
    rows = n_tok // LANES
    rb = 8
    pos2d = positions.reshape(rows, LANES)
    inv_freq = (ROPE_BASE ** (-jnp.arange(0, RET_DK, 2, dtype=F32) / RET_DK)).reshape(1, RET_DK // 2)
    out = jax.ShapeDtypeStruct((n_tok, RET_DK // 2), F32)
    return pl.pallas_call(
        _rope_kernel,
        grid=(rows // rb,),
        in_specs=[pl.BlockSpec((rb, LANES), lambda i: (i, 0)),
                  pl.BlockSpec((1, RET_DK // 2), lambda i: (0, 0))],
        out_specs=[pl.BlockSpec((rb * LANES, RET_DK // 2), lambda i: (i, 0))] * 2,
        out_shape=[out, out],
        compiler_params=_params(1),
        name="rope_tables",
    )(pos2d, inv_freq)


GATE_LO = 2 * GLA_QK + 2 * GLA_V
GATE_HI = GATE_LO + GLA_GATE_RANK


def _inproj_kernel(x_ref, g_ref, w_ref, wgr_ref, gup_ref, gb_ref, cos_ref, sin_ref, wout_ref,
                   proj_ref, loga_ref, wout_b_ref, h_ref, *, q_tile, k_tile):
    j = pl.program_id(1)

    def project(h):
        return lax.dot_general(h, w_ref[...], _NT, preferred_element_type=F32)

    @pl.when(j == 0)
    def _():
        wout_b_ref[...] = wout_ref[...].astype(wout_b_ref.dtype)
        hb = _rms(x_ref[...], g_ref[...]).astype(BF16)
        h_ref[...] = hb
        gr = lax.dot_general(hb, wgr_ref[...], _NT, preferred_element_type=F32)
        z = jnp.dot(gr.astype(BF16), gup_ref[...], preferred_element_type=F32) + gb_ref[...]
        log_sig = jnp.minimum(z, 0.0) - jnp.log(1.0 + jnp.exp(-jnp.abs(z)))
        loga_ref[...] = log_sig * (1.0 / GLA_TAU)
        proj_ref[...] = project(hb).astype(proj_ref.dtype)

    is_rotary = (j == q_tile) | (j == k_tile)

    @pl.when((j != 0) & jnp.logical_not(is_rotary))
    def _():
        proj_ref[...] = project(h_ref[...]).astype(proj_ref.dtype)

    @pl.when(is_rotary)
    def _():
        p = project(h_ref[...])
        scale = jnp.where(j == k_tile, RET_DK ** -0.5, 1.0).astype(F32)
        cos = cos_ref[...] * scale
        sin = sin_ref[...] * scale
        half = RET_DK // 2
        for h in range(RET_HEADS):
            lo = slice(h * RET_DK, h * RET_DK + half)
            hi = slice(h * RET_DK + half, (h + 1) * RET_DK)
            t1, t2 = p[:, lo], p[:, hi]
            proj_ref[:, lo] = (t1 * cos - t2 * sin).astype(proj_ref.dtype)
            proj_ref[:, hi] = (t2 * cos + t1 * sin).astype(proj_ref.dtype)


def _inproj(x, gain, w_t, w_gr_t, gate_up, gate_bias, cos, sin, w_out, layer, tm=1024):
    n_tok = x.shape[0]
    wout_rows = D_MODEL // (n_tok // tm)
    tn = RET_QK
    assert OFF_RQ % tn == 0 and OFF_RK % tn == 0 and GATE_LO % tn == 0 and OFF_RQ >= tn

    def w_rows(i, j):
        start = j * tn
        row = jnp.where(start < GATE_LO, start, start + GLA_GATE_RANK)
        return pl.multiple_of(row, GLA_GATE_RANK), 0

    return pl.pallas_call(
        functools.partial(_inproj_kernel, q_tile=OFF_RQ // tn, k_tile=OFF_RK // tn),
        grid=(n_tok // tm, D_PROJ // tn),
        in_specs=[pl.BlockSpec((tm, D_MODEL), lambda i, j: (i, 0)),
                  pl.BlockSpec((None, 1, D_MODEL), lambda i, j: (layer, 0, 0)),
                  pl.BlockSpec((pl.Element(tn), pl.Element(D_MODEL)), w_rows),
                  pl.BlockSpec((None, LANES, D_MODEL), lambda i, j: (layer, 0, 0)),
                  pl.BlockSpec((None, LANES, GLA_QK), lambda i, j: (layer, 0, 0)),
                  pl.BlockSpec((None, 1, GLA_QK), lambda i, j: (layer, 0, 0)),
                  pl.BlockSpec((tm, RET_DK // 2), lambda i, j: (i, 0)),
                  pl.BlockSpec((tm, RET_DK // 2), lambda i, j: (i, 0)),
                  pl.BlockSpec((None, wout_rows, D_MODEL), lambda i, j: (layer, i, 0))],
        out_specs=[pl.BlockSpec((tm, tn), lambda i, j: (i, j)),
                   pl.BlockSpec((tm, GLA_QK), lambda i, j: (i, 0)),
                   pl.BlockSpec((wout_rows, D_MODEL), lambda i, j: (i, 0))],
        out_shape=[jax.ShapeDtypeStruct((n_tok, D_PROJ), BF16),
                   jax.ShapeDtypeStruct((n_tok, GLA_QK), F32),
                   jax.ShapeDtypeStruct((D_MODEL, D_MODEL), BF16)],
        scratch_shapes=[pltpu.VMEM((tm, D_MODEL), BF16)],
        compiler_params=_params(2),
        name="norm_inproj",
    )(x, gain, w_t, w_gr_t, gate_up, gate_bias, cos, sin, w_out)


def _split3(x):
    hi = x.astype(BF16)
    r1 = x - hi.astype(F32)
    mid = r1.astype(BF16)
    lo = (r1 - mid.astype(F32)).astype(BF16)
    return hi, mid, lo


def _ret_gamma(h):
    return 1.0 - 2.0 ** (-5.0 - h)


def _ret_decay_tables(t_blk):
    assert t_blk == RET_DK == RET_DV
    log_gamma = np.log1p(-(2.0 ** (-5.0 - np.arange(RET_HEADS))))[:, None, None]
    idx = np.arange(t_blk, dtype=np.float64)
    rel = idx[:, None] - idx[None, :]
    decay = np.where(rel >= 0.0, np.exp(log_gamma * np.maximum(rel, 0.0)), 0.0)
    tail = np.broadcast_to(np.exp(log_gamma * (t_blk - 1.0 - idx[None, :, None])), decay.shape)
    inner = np.broadcast_to(np.exp(log_gamma * (idx[None, :, None] + 1.0)), decay.shape)
    return jnp.asarray(np.stack([decay, tail, inner], axis=1), dtype=F32)


def _silu(g):
    half_g = 0.5 * g
    return half_g + half_g * jnp.tanh(half_g)


def _mix_block(proj_ref, loga_ref, gno_ref, rgain_ref, rbias_ref, dec_ref, mix_ref, sg_ref, sr_ref):
    t_blk = MIX_BLOCK
    n_chunk = t_blk // GLA_CHUNK

    row = lax.broadcasted_iota(jnp.int32, (t_blk, t_blk), 0)
    col = lax.broadcasted_iota(jnp.int32, (t_blk, t_blk), 1)
    causal = col <= row
    chunk_shift = int(math.log2(GLA_CHUNK))
    gla_mask = causal & ((row >> chunk_shift) == (col >> chunk_shift))

    cum = gla_mask.astype(BF16)
    la_hi, la_mid, la_lo = _split3(loga_ref[...])
    b_all = (jnp.dot(cum, la_hi, preferred_element_type=F32)
             + jnp.dot(cum, la_mid, preferred_element_type=F32)
             + jnp.dot(cum, la_lo, preferred_element_type=F32))

    for h in range(GLA_HEADS):
        q = proj_ref[:, OFF_GQ + h * GLA_DK:OFF_GQ + (h + 1) * GLA_DK].astype(F32)
        k = proj_ref[:, OFF_GK + h * GLA_DK:OFF_GK + (h + 1) * GLA_DK].astype(F32)
        v = proj_ref[:, OFF_GV + h * GLA_DV:OFF_GV + (h + 1) * GLA_DV]
        g = proj_ref[:, OFF_GG + h * GLA_DV:OFF_GG + (h + 1) * GLA_DV].astype(F32)
        b = b_all[:, h * GLA_DK:(h + 1) * GLA_DK]
        b_last = [b[(c + 1) * GLA_CHUNK - 1:(c + 1) * GLA_CHUNK, :] for c in range(n_chunk)]
        b_last_rows = jnp.concatenate(
            [jnp.broadcast_to(bl, (GLA_CHUNK, GLA_DK)) for bl in b_last], axis=0)
        q_dec = (q * (GLA_DK ** -0.5) * jnp.exp(b)).astype(BF16)
        k_dec = (k * jnp.exp(-b)).astype(BF16)
        k_tail = (k * jnp.exp(b_last_rows - b)).astype(BF16)

        s = lax.dot_general(q_dec, k_dec, _NT, preferred_element_type=F32)
        s = jnp.where(gla_mask, s, 0.0).astype(BF16)
        o_intra = jnp.dot(s, v, preferred_element_type=F32)

        state = sg_ref[h]
        o_parts = []
        for c in range(n_chunk):
            sl = slice(c * GLA_CHUNK, (c + 1) * GLA_CHUNK)
            o_parts.append(o_intra[sl] + jnp.dot(q_dec[sl], state.astype(BF16),
                                                 preferred_element_type=F32))
            local = lax.dot_general(k_tail[sl], v[sl], _TN, preferred_element_type=F32)
            dcol = jnp.broadcast_to(jnp.exp(b_last[c]), (GLA_DK, GLA_DK)).T
            state = jnp.concatenate([dcol, dcol], axis=1) * state + local
        sg_ref[h] = state

        o = jnp.concatenate(o_parts, axis=0)
        y = o * lax.rsqrt(jnp.mean(o * o, axis=-1, keepdims=True) + EPS)
        y = y * gno_ref[:, h * GLA_DV:(h + 1) * GLA_DV]
        y = y * _silu(g)
        mix_ref[:, h * GLA_DV:(h + 1) * GLA_DV] = y.astype(mix_ref.dtype)

    for h in range(RET_HEADS):
        qb = proj_ref[:, OFF_RQ + h * RET_DK:OFF_RQ + (h + 1) * RET_DK]
        kb = proj_ref[:, OFF_RK + h * RET_DK:OFF_RK + (h + 1) * RET_DK]
        v = proj_ref[:, OFF_RV + h * RET_DV:OFF_RV + (h + 1) * RET_DV]
        g = proj_ref[:, OFF_RG + h * RET_DV:OFF_RG + (h + 1) * RET_DV].astype(F32)
        s = lax.dot_general(qb, kb, _NT, preferred_element_type=F32) * dec_ref[h, 0]
        o_intra = jnp.dot(s.astype(BF16), v, preferred_element_type=F32)
        k_tail = (kb.astype(F32) * dec_ref[h, 1]).astype(BF16)
        local = lax.dot_general(k_tail, v, _TN, preferred_element_type=F32)
        state = sr_ref[h]
        o_inter = jnp.dot(qb, state.astype(BF16), preferred_element_type=F32)
        o = o_intra + o_inter * dec_ref[h, 2]
        sr_ref[h] = _ret_gamma(h) ** t_blk * state + local

        mu = jnp.mean(o, axis=-1, keepdims=True)
        cen = o - mu
        y = cen * lax.rsqrt(jnp.mean(cen * cen, axis=-1, keepdims=True) + EPS)
        y = y * rgain_ref[:, h * RET_DV:(h + 1) * RET_DV] + rbias_ref[:, h * RET_DV:(h + 1) * RET_DV]
        y = y * _silu(g)
        mix_ref[:, GLA_V + h * RET_DV:GLA_V + (h + 1) * RET_DV] = y.astype(mix_ref.dtype)


def _mixer_outproj_kernel(proj_ref, loga_ref, gno_ref, rgain_ref, rbias_ref, dec_ref, x_ref, wout_ref,
                          wup_ref, wdown_ref, *rest, blocks_per_seq, cast_next_win):
    if cast_next_win:
        win_ref, out_ref, wup_b_ref, wdown_b_ref, win_b_ref, sg_ref, sr_ref, mix_a, mix_b = rest
    else:
        out_ref, wup_b_ref, wdown_b_ref, sg_ref, sr_ref, mix_a, mix_b = rest
    t = pl.program_id(0)

    @pl.when(lax.rem(t, blocks_per_seq) == 0)
    def _():
        sg_ref[...] = jnp.zeros_like(sg_ref)
        sr_ref[...] = jnp.zeros_like(sr_ref)

    @pl.when(t == 0)
    def _():
        mix_b[...] = jnp.zeros_like(mix_b)

    def step(src, dst):
        for k in range(wup_b_ref.shape[0]):
            cols = slice(k * MLP_FF_TILE, (k + 1) * MLP_FF_TILE)
            wup_b_ref[k] = wup_ref[:, cols].astype(wup_b_ref.dtype)
        wdown_b_ref[...] = wdown_ref[...].astype(wdown_b_ref.dtype)
        if cast_next_win:
            win_b_ref[...] = win_ref[0].astype(win_b_ref.dtype)
        _mix_block(proj_ref, loga_ref, gno_ref, rgain_ref, rbias_ref, dec_ref, dst, sg_ref, sr_ref)
        out_ref[...] = x_ref[...] + jnp.dot(src[...], wout_ref[...], preferred_element_type=F32)

    parity = lax.rem(t, 2)

    @pl.when(parity == 0)
    def _():
        step(mix_b, mix_a)

    @pl.when(parity == 1)
    def _():
        step(mix_a, mix_b)


def _mixers_outproj(proj, loga, gla_norm, ret_gain, ret_bias, x, w_out, w_up, w_down, next_win_t, layer, seq):
    n_tok = proj.shape[0]
    t_blk = MIX_BLOCK
    n_blk = n_tok // t_blk
    n_ff = D_FF // MLP_FF_TILE
    cur = lambda t: (jnp.minimum(t, n_blk - 1), 0)
    prev = lambda t: (jnp.maximum(t - 1, 0), 0)
    const = lambda t: (layer, 0, 0)
    slab = lambda t: (layer, jnp.minimum(t, n_blk - 1), 0)
    up_rows, down_rows = D_MODEL // n_blk, D_FF // n_blk
    cast_next_win = next_win_t is not None

    in_specs = [pl.BlockSpec((t_blk, D_PROJ), cur),
                pl.BlockSpec((t_blk, GLA_QK), cur),
                pl.BlockSpec((None, 1, GLA_V), const),
                pl.BlockSpec((None, 1, RET_V), const),
                pl.BlockSpec((None, 1, RET_V), const),
                pl.BlockSpec((RET_HEADS, 3, t_blk, t_blk), lambda t: (0, 0, 0, 0)),
                pl.BlockSpec((t_blk, D_MODEL), prev),
                pl.BlockSpec((D_MODEL, D_MODEL), lambda t: (0, 0)),
                pl.BlockSpec((None, up_rows, D_FF), slab),
                pl.BlockSpec((None, down_rows, D_MODEL), slab)]
    out_specs = [pl.BlockSpec((t_blk, D_MODEL), prev),
                 pl.BlockSpec((n_ff, up_rows, MLP_FF_TILE), lambda t: (0, jnp.minimum(t, n_blk - 1), 0)),
                 pl.BlockSpec((down_rows, D_MODEL), cur)]
    out_shape = [jax.ShapeDtypeStruct((n_tok, D_MODEL), F32),
                 jax.ShapeDtypeStruct((n_ff, D_MODEL, MLP_FF_TILE), BF16),
                 jax.ShapeDtypeStruct((D_FF, D_MODEL), BF16)]
    operands = [proj, loga, gla_norm, ret_gain, ret_bias, _ret_decay_tables(t_blk), x, w_out, w_up, w_down]
    if cast_next_win:
        w_in_t, next_layer = next_win_t
        d_in = w_in_t.shape[1]
        last = d_in - WIN_SLAB
        assert last % GLA_GATE_RANK == 0 and pl.cdiv(d_in, WIN_SLAB) <= n_blk
        win_row = lambda t: pl.multiple_of(jnp.minimum(t * WIN_SLAB, last), GLA_GATE_RANK)
        in_specs.append(pl.BlockSpec((pl.Element(1), pl.Element(WIN_SLAB), pl.Element(D_MODEL)),
                                     lambda t: (next_layer, win_row(t), 0)))
        out_specs.append(pl.BlockSpec((pl.Element(WIN_SLAB), pl.Element(D_MODEL)), lambda t: (win_row(t), 0)))
        out_shape.append(jax.ShapeDtypeStruct((d_in, D_MODEL), BF16))
        operands.append(w_in_t)

    return pl.pallas_call(
        functools.partial(_mixer_outproj_kernel, blocks_per_seq=seq // t_blk, cast_next_win=cast_next_win),
        grid=(n_blk + 1,),
        in_specs=in_specs,
        out_specs=out_specs,
        out_shape=out_shape,
        scratch_shapes=[pltpu.VMEM((GLA_HEADS, GLA_DK, GLA_DV), F32),
                        pltpu.VMEM((RET_HEADS, RET_DK, RET_DV), F32),
                        pltpu.VMEM((t_blk, D_MODEL), BF16),
                        pltpu.VMEM((t_blk, D_MODEL), BF16)],
        compiler_params=_params(1),
        name="mixers_outproj",
    )(*operands)


def _mlp_kernel(x_ref, g_ref, wu_ref, wd_ref, fg_ref, o_ref, h_ref, *, apply_final_norm):
    k = pl.program_id(1)

    def ff_tile(h):
        u = jnp.dot(h, wu_ref[...], preferred_element_type=F32)
        a = jnp.square(jnp.maximum(u, 0.0)).astype(BF16)
        return jnp.dot(a, wd_ref[...], preferred_element_type=F32)

    @pl.when(k == 0)
    def _():
        x = x_ref[...]
        h = _rms(x, g_ref[...]).astype(BF16)
        h_ref[...] = h
        o_ref[...] = x + ff_tile(h)

    @pl.when(k != 0)
    def _():
        o_ref[...] += ff_tile(h_ref[...])

    if apply_final_norm:
        @pl.when(k == pl.num_programs(1) - 1)
        def _():
            o_ref[...] = _rms(o_ref[...], fg_ref[...])


def _mlp(x, gain, w_up, w_down, final_gain, layer, apply_final_norm, tm=1024):
    n_tok = x.shape[0]
    tf = MLP_FF_TILE
    return pl.pallas_call(
        functools.partial(_mlp_kernel, apply_final_norm=apply_final_norm),
        grid=(n_tok // tm, D_FF // tf),
        in_specs=[pl.BlockSpec((tm, D_MODEL), lambda i, k: (i, 0)),
                  pl.BlockSpec((None, 1, D_MODEL), lambda i, k: (layer, 0, 0)),
                  pl.BlockSpec((None, D_MODEL, tf), lambda i, k: (k, 0, 0)),
                  pl.BlockSpec((tf, D_MODEL), lambda i, k: (k, 0)),
                  pl.BlockSpec((1, D_MODEL), lambda i, k: (0, 0))],
        out_specs=pl.BlockSpec((tm, D_MODEL), lambda i, k: (i, 0)),
        out_shape=jax.ShapeDtypeStruct((n_tok, D_MODEL), F32),
        scratch_shapes=[pltpu.VMEM((tm, D_MODEL), BF16)],
        compiler_params=_params(2),
        name="norm_mlp_residual",
    )(x, gain, w_up, w_down, final_gain)


def kernel(x, positions, attn_norm, w_in, gla_gate_up, gla_gate_bias, gla_out_norm,
           ret_norm_gain, ret_norm_bias, w_out, mlp_norm, w_up, w_down, final_norm):
    batch, seq, d_model = x.shape
    depth = w_in.shape[0]
    assert d_model == D_MODEL and seq % MIX_BLOCK == 0
    n_tok = batch * seq
    xf = x.reshape(n_tok, d_model)

    cos, sin = _rope_tables(positions)

    pad = LANES - GLA_GATE_RANK
    w_in_t = jnp.swapaxes(w_in, 1, 2)
    w_gr_t = jnp.pad(w_in[:, :, GATE_LO:GATE_HI].swapaxes(1, 2), ((0, 0), (0, pad), (0, 0))).astype(BF16)
    gate_up = jnp.pad(gla_gate_up, ((0, 0), (0, pad), (0, 0))).astype(BF16)
    row3 = lambda a: a.reshape(depth, 1, -1)
    attn_g, gate_b, mlp_g = row3(attn_norm), row3(gla_gate_bias), row3(mlp_norm)
    gla_g, ret_g, ret_b = row3(gla_out_norm), row3(ret_norm_gain), row3(ret_norm_bias)
    final_g = final_norm.reshape(1, -1)

    w_t = w_in_t[0].astype(BF16)
    for l in range(depth):
        last = l == depth - 1
        proj, loga, w_out_b = _inproj(xf, attn_g, w_t, w_gr_t, gate_up, gate_b, cos, sin, w_out, l)
        outs = _mixers_outproj(proj, loga, gla_g, ret_g, ret_b, xf, w_out_b, w_up, w_down,
                               None if last else (w_in_t, l + 1), l, seq)
        xf, w_up_b, w_down_b = outs[:3]
        if not last:
            w_t = outs[3]
        xf = _mlp(xf, mlp_g, w_up_b, w_down_b, final_g, l, apply_final_norm=last)
    return xf.reshape(batch, seq, d_model)
```

```python
import functools
import math

import jax
import jax.numpy as jnp
import numpy as np
from jax import lax
from jax.experimental import pallas as pl
from jax.experimental.pallas import tpu as pltpu

D_MODEL = 2048
GLA_HEADS = 4
GLA_DK = 128
GLA_DV = 256
GLA_GATE_RANK = 16
GLA_TAU = 16.0
GLA_CHUNK = 64
RET_HEADS = 4
RET_DK = 256
RET_DV = 256
ROPE_BASE = 10000.0
D_FF = 4 * D_MODEL
EPS = 1e-6

GLA_QK = GLA_HEADS * GLA_DK
GLA_V = GLA_HEADS * GLA_DV
RET_QK = RET_HEADS * RET_DK
RET_V = RET_HEADS * RET_DV
D_PROJ = 2 * GLA_QK + 2 * GLA_V + 2 * RET_QK + 2 * RET_V

OFF_GQ = 0
OFF_GK = OFF_GQ + GLA_QK
OFF_GV = OFF_GK + GLA_QK
OFF_GG = OFF_GV + GLA_V
OFF_RQ = OFF_GG + GLA_V
OFF_RK = OFF_RQ + RET_QK
OFF_RV = OFF_RK + RET_QK
OFF_RG = OFF_RV + RET_V

LANES = 128
MIX_BLOCK = 256
MLP_FF_TILE = 1024
WIN_SLAB = 128
VMEM_LIMIT = 60 * 1024 * 1024

F32 = jnp.float32
BF16 = jnp.bfloat16

_NT = (((1,), (1,)), ((), ()))
_TN = (((0,), (0,)), ((), ()))


def _params(n_axes):
    return pltpu.CompilerParams(
        dimension_semantics=("arbitrary",) * n_axes, vmem_limit_bytes=VMEM_LIMIT)


def _rms(x, gain):
    return x * lax.rsqrt(jnp.mean(x * x, axis=-1, keepdims=True) + EPS) * gain


def _rope_kernel(pos_ref, invf_ref, cos_ref, sin_ref):
    rows = pos_ref.shape[0]
    invf = invf_ref[...]
    for r in range(rows):
        p = pos_ref[r:r + 1, :].astype(F32)
        col = jnp.broadcast_to(p, (LANES, LANES)).T
        ang = col * invf
        cos_ref[r * LANES:(r + 1) * LANES, :] = jnp.cos(ang)
        sin_ref[r * LANES:(r + 1) * LANES, :] = jnp.sin(ang)


def _rope_tables(positions):
    n_tok = positions.size
    rows = n_tok // LANES
    rb = 8
    pos2d = positions.reshape(rows, LANES)
    inv_freq = (ROPE_BASE ** (-jnp.arange(0, RET_DK, 2, dtype=F32) / RET_DK)).reshape(1, RET_DK // 2)
    out = jax.ShapeDtypeStruct((n_tok, RET_DK // 2), F32)
    return pl.pallas_call(
        _rope_kernel,
        grid=(rows // rb,),
        in_specs=[pl.BlockSpec((rb, LANES), lambda i: (i, 0)),
                  pl.BlockSpec((1, RET_DK // 2), lambda i: (0, 0))],
        out_specs=[pl.BlockSpec((rb * LANES, RET_DK // 2), lambda i: (i, 0))] * 2,
        out_shape=[out, out],
        compiler_params=_params(1),
        name="rope_tables",
    )(pos2d, inv_freq)


GATE_LO = 2 * GLA_QK + 2 * GLA_V
GATE_HI = GATE_LO + GLA_GATE_RANK


def _inproj_kernel(x_ref, g_ref, w_ref, wgr_ref, gup_ref, gb_ref, cos_ref, sin_ref, wout_ref,
                   proj_ref, loga_ref, wout_b_ref, h_ref, *, q_tile, k_tile):
    j = pl.program_id(1)

    def project(h):
        return lax.dot_general(h, w_ref[...], _NT, preferred_element_type=F32)

    @pl.when(j == 0)
    def _():
        wout_b_ref[...] = wout_ref[...].astype(wout_b_ref.dtype)
        hb = _rms(x_ref[...], g_ref[...]).astype(BF16)
        h_ref[...] = hb
        gr = lax.dot_general(hb, wgr_ref[...], _NT, preferred_element_type=F32)
        z = jnp.dot(gr.astype(BF16), gup_ref[...], preferred_element_type=F32) + gb_ref[...]
        log_sig = jnp.minimum(z, 0.0) - jnp.log(1.0 + jnp.exp(-jnp.abs(z)))
        loga_ref[...] = log_sig * (1.0 / GLA_TAU)
        proj_ref[...] = project(hb).astype(proj_ref.dtype)

    is_rotary = (j == q_tile) | (j == k_tile)

    @pl.when((j != 0) & jnp.logical_not(is_rotary))
    def _():
        proj_ref[...] = project(h_ref[...]).astype(proj_ref.dtype)

    @pl.when(is_rotary)
    def _():
        p = project(h_ref[...])
        scale = jnp.where(j == k_tile, RET_DK ** -0.5, 1.0).astype(F32)
        cos = cos_ref[...] * scale
        sin = sin_ref[...] * scale
        half = RET_DK // 2
        for h in range(RET_HEADS):
            lo = slice(h * RET_DK, h * RET_DK + half)
            hi = slice(h * RET_DK + half, (h + 1) * RET_DK)
            t1, t2 = p[:, lo], p[:, hi]
            proj_ref[:, lo] = (t1 * cos - t2 * sin).astype(proj_ref.dtype)
            proj_ref[:, hi] = (t2 * cos + t1 * sin).astype(proj_ref.dtype)


def _inproj(x, gain, w_t, w_gr_t, gate_up, gate_bias, cos, sin, w_out, layer, tm=1024):
    n_tok = x.shape[0]
    wout_rows = D_MODEL // (n_tok // tm)
    tn = RET_QK
    assert OFF_RQ % tn == 0 and OFF_RK % tn == 0 and GATE_LO % tn == 0 and OFF_RQ >= tn

    def w_rows(i, j):
        start = j * tn
        row = jnp.where(start < GATE_LO, start, start + GLA_GATE_RANK)
        return pl.multiple_of(row, GLA_GATE_RANK), 0

    return pl.pallas_call(
        functools.partial(_inproj_kernel, q_tile=OFF_RQ // tn, k_tile=OFF_RK // tn),
        grid=(n_tok // tm, D_PROJ // tn),
        in_specs=[pl.BlockSpec((tm, D_MODEL), lambda i, j: (i, 0)),
                  pl.BlockSpec((None, 1, D_MODEL), lambda i, j: (layer, 0, 0)),
                  pl.BlockSpec((pl.Element(tn), pl.Element(D_MODEL)), w_rows),
                  pl.BlockSpec((None, LANES, D_MODEL), lambda i, j: (layer, 0, 0)),
                  pl.BlockSpec((None, LANES, GLA_QK), lambda i, j: (layer, 0, 0)),
                  pl.BlockSpec((None, 1, GLA_QK), lambda i, j: (layer, 0, 0)),
                  pl.BlockSpec((tm, RET_DK // 2), lambda i, j: (i, 0)),
                  pl.BlockSpec((tm, RET_DK // 2), lambda i, j: (i, 0)),
                  pl.BlockSpec((None, wout_rows, D_MODEL), lambda i, j: (layer, i, 0))],
        out_specs=[pl.BlockSpec((tm, tn), lambda i, j: (i, j)),
                   pl.BlockSpec((tm, GLA_QK), lambda i, j: (i, 0)),
                   pl.BlockSpec((wout_rows, D_MODEL), lambda i, j: (i, 0))],
        out_shape=[jax.ShapeDtypeStruct((n_tok, D_PROJ), BF16),
                   jax.ShapeDtypeStruct((n_tok, GLA_QK), F32),
                   jax.ShapeDtypeStruct((D_MODEL, D_MODEL), BF16)],
        scratch_shapes=[pltpu.VMEM((tm, D_MODEL), BF16)],
        compiler_params=_params(2),
        name="norm_inproj",
    )(x, gain, w_t, w_gr_t, gate_up, gate_bias, cos, sin, w_out)


def _split3(x):
    hi = x.astype(BF16)
    r1 = x - hi.astype(F32)
    mid = r1.astype(BF16)
    lo = (r1 - mid.astype(F32)).astype(BF16)
    return hi, mid, lo


def _ret_gamma(h):
    return 1.0 - 2.0 ** (-5.0 - h)


def _ret_decay_tables(t_blk):
    assert t_blk == RET_DK == RET_DV
    log_gamma = np.log1p(-(2.0 ** (-5.0 - np.arange(RET_HEADS))))[:, None, None]
    idx = np.arange(t_blk, dtype=np.float64)
    rel = idx[:, None] - idx[None, :]
    decay = np.where(rel >= 0.0, np.exp(log_gamma * np.maximum(rel, 0.0)), 0.0)
    tail = np.broadcast_to(np.exp(log_gamma * (t_blk - 1.0 - idx[None, :, None])), decay.shape)
    inner = np.broadcast_to(np.exp(log_gamma * (idx[None, :, None] + 1.0)), decay.shape)
    return jnp.asarray(np.stack([decay, tail, inner], axis=1), dtype=F32)


def _silu(g):
    half_g = 0.5 * g
    return half_g + half_g * jnp.tanh(half_g)


def _mix_block(proj_ref, loga_ref, gno_ref, rgain_ref, rbias_ref, dec_ref, mix_ref, sg_ref, sr_ref):
    t_blk = MIX_BLOCK
    n_chunk = t_blk // GLA_CHUNK

    row = lax.broadcasted_iota(jnp.int32, (t_blk, t_blk), 0)
    col = lax.broadcasted_iota(jnp.int32, (t_blk, t_blk), 1)
    causal = col <= row
    chunk_shift = int(math.log2(GLA_CHUNK))
    gla_mask = causal & ((row >> chunk_shift) == (col >> chunk_shift))

    cum = gla_mask.astype(BF16)
    la_hi, la_mid, la_lo = _split3(loga_ref[...])
    b_all = (jnp.dot(cum, la_hi, preferred_element_type=F32)
             + jnp.dot(cum, la_mid, preferred_element_type=F32)
             + jnp.dot(cum, la_lo, preferred_element_type=F32))

    for h in range(GLA_HEADS):
        q = proj_ref[:, OFF_GQ + h * GLA_DK:OFF_GQ + (h + 1) * GLA_DK].astype(F32)
        k = proj_ref[:, OFF_GK + h * GLA_DK:OFF_GK + (h + 1) * GLA_DK].astype(F32)
        v = proj_ref[:, OFF_GV + h * GLA_DV:OFF_GV + (h + 1) * GLA_DV]
        g = proj_ref[:, OFF_GG + h * GLA_DV:OFF_GG + (h + 1) * GLA_DV].astype(F32)
        b = b_all[:, h * GLA_DK:(h + 1) * GLA_DK]
        b_last = [b[(c + 1) * GLA_CHUNK - 1:(c + 1) * GLA_CHUNK, :] for c in range(n_chunk)]
        b_last_rows = jnp.concatenate(
            [jnp.broadcast_to(bl, (GLA_CHUNK, GLA_DK)) for bl in b_last], axis=0)
        q_dec = (q * (GLA_DK ** -0.5) * jnp.exp(b)).astype(BF16)
        k_dec = (k * jnp.exp(-b)).astype(BF16)
        k_tail = (k * jnp.exp(b_last_rows - b)).astype(BF16)

        s = lax.dot_general(q_dec, k_dec, _NT, preferred_element_type=F32)
        s = jnp.where(gla_mask, s, 0.0).astype(BF16)
        o_intra = jnp.dot(s, v, preferred_element_type=F32)

        state = sg_ref[h]
        o_parts = []
        for c in range(n_chunk):
            sl = slice(c * GLA_CHUNK, (c + 1) * GLA_CHUNK)
            o_parts.append(o_intra[sl] + jnp.dot(q_dec[sl], state.astype(BF16),
                                                 preferred_element_type=F32))
            local = lax.dot_general(k_tail[sl], v[sl], _TN, preferred_element_type=F32)
            dcol = jnp.broadcast_to(jnp.exp(b_last[c]), (GLA_DK, GLA_DK)).T
            state = jnp.concatenate([dcol, dcol], axis=1) * state + local
        sg_ref[h] = state

        o = jnp.concatenate(o_parts, axis=0)
        y = o * lax.rsqrt(jnp.mean(o * o, axis=-1, keepdims=True) + EPS)
        y = y * gno_ref[:, h * GLA_DV:(h + 1) * GLA_DV]
        y = y * _silu(g)
        mix_ref[:, h * GLA_DV:(h + 1) * GLA_DV] = y.astype(mix_ref.dtype)

    for h in range(RET_HEADS):
        qb = proj_ref[:, OFF_RQ + h * RET_DK:OFF_RQ + (h + 1) * RET_DK]
        kb = proj_ref[:, OFF_RK + h * RET_DK:OFF_RK + (h + 1) * RET_DK]
        v = proj_ref[:, OFF_RV + h * RET_DV:OFF_RV + (h + 1) * RET_DV]
        g = proj_ref[:, OFF_RG + h * RET_DV:OFF_RG + (h + 1) * RET_DV].astype(F32)
        s = lax.dot_general(qb, kb, _NT, preferred_element_type=F32) * dec_ref[h, 0]
        o_intra = jnp.dot(s.astype(BF16), v, preferred_element_type=F32)
        k_tail = (kb.astype(F32) * dec_ref[h, 1]).astype(BF16)
        local = lax.dot_general(k_tail, v, _TN, preferred_element_type=F32)
        state = sr_ref[h]
        o_inter = jnp.dot(qb, state.astype(BF16), preferred_element_type=F32)
        o = o_intra + o_inter * dec_ref[h, 2]
        sr_ref[h] = _ret_gamma(h) ** t_blk * state + local

        mu = jnp.mean(o, axis=-1, keepdims=True)
        cen = o - mu
        y = cen * lax.rsqrt(jnp.mean(cen * cen, axis=-1, keepdims=True) + EPS)
        y = y * rgain_ref[:, h * RET_DV:(h + 1) * RET_DV] + rbias_ref[:, h * RET_DV:(h + 1) * RET_DV]
        y = y * _silu(g)
        mix_ref[:, GLA_V + h * RET_DV:GLA_V + (h + 1) * RET_DV] = y.astype(mix_ref.dtype)


def _mixer_outproj_kernel(proj_ref, loga_ref, gno_ref, rgain_ref, rbias_ref, dec_ref, x_ref, wout_ref,
                          wup_ref, wdown_ref, *rest, blocks_per_seq, cast_next_win,
                          win_full_slabs=0, win_tail_rows=0):
    if cast_next_win:
        win_ref, out_ref, wup_b_ref, wdown_b_ref, win_b_ref, sg_ref, sr_ref, mix_a, mix_b = rest
    else:
        out_ref, wup_b_ref, wdown_b_ref, sg_ref, sr_ref, mix_a, mix_b = rest
    t = pl.program_id(0)

    @pl.when(lax.rem(t, blocks_per_seq) == 0)
    def _():
        sg_ref[...] = jnp.zeros_like(sg_ref)
        sr_ref[...] = jnp.zeros_like(sr_ref)

    @pl.when(t == 0)
    def _():
        mix_b[...] = jnp.zeros_like(mix_b)

    if cast_next_win:
        @pl.when(t < win_full_slabs)
        def _():
            win_b_ref[...] = win_ref[0].astype(win_b_ref.dtype)

        @pl.when(t >= win_full_slabs)
        def _():
            win_b_ref[...] = jnp.zeros_like(win_b_ref)
            win_b_ref[:win_tail_rows, :] = win_ref[0, WIN_SLAB - win_tail_rows:, :].astype(win_b_ref.dtype)

    def step(src, dst):
        for k in range(wup_b_ref.shape[0]):
            cols = slice(k * MLP_FF_TILE, (k + 1) * MLP_FF_TILE)
            wup_b_ref[k] = wup_ref[:, cols].astype(wup_b_ref.dtype)
        wdown_b_ref[...] = wdown_ref[...].astype(wdown_b_ref.dtype)
        _mix_block(proj_ref, loga_ref, gno_ref, rgain_ref, rbias_ref, dec_ref, dst, sg_ref, sr_ref)
        out_ref[...] = x_ref[...] + jnp.dot(src[...], wout_ref[...], preferred_element_type=F32)

    parity = lax.rem(t, 2)

    @pl.when(parity == 0)
    def _():
        step(mix_b, mix_a)

    @pl.when(parity == 1)
    def _():
        step(mix_a, mix_b)


def _mixers_outproj(proj, loga, gla_norm, ret_gain, ret_bias, x, w_out, w_up, w_down, next_win_t, layer, seq):
    n_tok = proj.shape[0]
    t_blk = MIX_BLOCK
    n_blk = n_tok // t_blk
    n_ff = D_FF // MLP_FF_TILE
    cur = lambda t: (jnp.minimum(t, n_blk - 1), 0)
    prev = lambda t: (jnp.maximum(t - 1, 0), 0)
    const = lambda t: (layer, 0, 0)
    slab = lambda t: (layer, jnp.minimum(t, n_blk - 1), 0)
    up_rows, down_rows = D_MODEL // n_blk, D_FF // n_blk
    cast_next_win = next_win_t is not None
    static = dict(blocks_per_seq=seq // t_blk, cast_next_win=cast_next_win)

    in_specs = [pl.BlockSpec((t_blk, D_PROJ), cur),
                pl.BlockSpec((t_blk, GLA_QK), cur),
                pl.BlockSpec((None, 1, GLA_V), const),
                pl.BlockSpec((None, 1, RET_V), const),
                pl.BlockSpec((None, 1, RET_V), const),
                pl.BlockSpec((RET_HEADS, 3, t_blk, t_blk), lambda t: (0, 0, 0, 0)),
                pl.BlockSpec((t_blk, D_MODEL), prev),
                pl.BlockSpec((D_MODEL, D_MODEL), lambda t: (0, 0)),
                pl.BlockSpec((None, up_rows, D_FF), slab),
                pl.BlockSpec((None, down_rows, D_MODEL), slab)]
    out_specs = [pl.BlockSpec((t_blk, D_MODEL), prev),
                 pl.BlockSpec((n_ff, up_rows, MLP_FF_TILE), lambda t: (0, jnp.minimum(t, n_blk - 1), 0)),
                 pl.BlockSpec((down_rows, D_MODEL), cur)]
    out_shape = [jax.ShapeDtypeStruct((n_tok, D_MODEL), F32),
                 jax.ShapeDtypeStruct((n_ff, D_MODEL, MLP_FF_TILE), BF16),
                 jax.ShapeDtypeStruct((D_FF, D_MODEL), BF16)]
    operands = [proj, loga, gla_norm, ret_gain, ret_bias, _ret_decay_tables(t_blk), x, w_out, w_up, w_down]
    if cast_next_win:
        w_in_t, next_layer = next_win_t
        d_in = w_in_t.shape[1]
        win_full_slabs, win_tail_rows = divmod(d_in, WIN_SLAB)
        n_slabs = win_full_slabs + (win_tail_rows > 0)
        last = d_in - WIN_SLAB
        assert last % GLA_GATE_RANK == 0 and win_tail_rows % GLA_GATE_RANK == 0 and n_slabs <= n_blk
        win_row = lambda t: pl.multiple_of(jnp.minimum(t * WIN_SLAB, last), GLA_GATE_RANK)
        in_specs.append(pl.BlockSpec((pl.Element(1), pl.Element(WIN_SLAB), pl.Element(D_MODEL)),
                                     lambda t: (next_layer, win_row(t), 0)))
        out_specs.append(pl.BlockSpec((WIN_SLAB, D_MODEL), lambda t: (jnp.minimum(t, n_slabs - 1), 0)))
        out_shape.append(jax.ShapeDtypeStruct((n_slabs * WIN_SLAB, D_MODEL), BF16))
        static.update(win_full_slabs=win_full_slabs, win_tail_rows=win_tail_rows)
        operands.append(w_in_t)

    return pl.pallas_call(
        functools.partial(_mixer_outproj_kernel, **static),
        grid=(n_blk + 1,),
        in_specs=in_specs,
        out_specs=out_specs,
        out_shape=out_shape,
        scratch_shapes=[pltpu.VMEM((GLA_HEADS, GLA_DK, GLA_DV), F32),
                        pltpu.VMEM((RET_HEADS, RET_DK, RET_DV), F32),
                        pltpu.VMEM((t_blk, D_MODEL), BF16),
                        pltpu.VMEM((t_blk, D_MODEL), BF16)],
        compiler_params=_params(1),
        name="mixers_outproj",
    )(*operands)


def _mlp_kernel(x_ref, g_ref, wu_ref, wd_ref, fg_ref, o_ref, h_ref, *, apply_final_norm):
    k = pl.program_id(1)

    def ff_tile(h):
        u = jnp.dot(h, wu_ref[...], preferred_element_type=F32)
        a = jnp.square(jnp.maximum(u, 0.0)).astype(BF16)
        return jnp.dot(a, wd_ref[...], preferred_element_type=F32)

    @pl.when(k == 0)
    def _():
        x = x_ref[...]
        h = _rms(x, g_ref[...]).astype(BF16)
        h_ref[...] = h
        o_ref[...] = x + ff_tile(h)

    @pl.when(k != 0)
    def _():
        o_ref[...] += ff_tile(h_ref[...])

    if apply_final_norm:
        @pl.when(k == pl.num_programs(1) - 1)
        def _():
            o_ref[...] = _rms(o_ref[...], fg_ref[...])


def _mlp(x, gain, w_up, w_down, final_gain, layer, apply_final_norm, tm=1024):
    n_tok = x.shape[0]
    tf = MLP_FF_TILE
    return pl.pallas_call(
        functools.partial(_mlp_kernel, apply_final_norm=apply_final_norm),
        grid=(n_tok // tm, D_FF // tf),
        in_specs=[pl.BlockSpec((tm, D_MODEL), lambda i, k: (i, 0)),
                  pl.BlockSpec((None, 1, D_MODEL), lambda i, k: (layer, 0, 0)),
                  pl.BlockSpec((None, D_MODEL, tf), lambda i, k: (k, 0, 0)),
                  pl.BlockSpec((tf, D_MODEL), lambda i, k: (k, 0)),
                  pl.BlockSpec((1, D_MODEL), lambda i, k: (0, 0))],
        out_specs=pl.BlockSpec((tm, D_MODEL), lambda i, k: (i, 0)),
        out_shape=jax.ShapeDtypeStruct((n_tok, D_MODEL), F32),
        scratch_shapes=[pltpu.VMEM((tm, D_MODEL), BF16)],
        compiler_params=_params(2),
        name="norm_mlp_residual",
    )(x, gain, w_up, w_down, final_gain)


def kernel(x, positions, attn_norm, w_in, gla_gate_up, gla_gate_bias, gla_out_norm,
           ret_norm_gain, ret_norm_bias, w_out, mlp_norm, w_up, w_down, final_norm):
    batch, seq, d_model = x.shape
    depth = w_in.shape[0]
    assert d_model == D_MODEL and seq % MIX_BLOCK == 0
    n_tok = batch * seq
    xf = x.reshape(n_tok, d_model)

    cos, sin = _rope_tables(positions)

    pad = LANES - GLA_GATE_RANK
    w_in_t = jnp.swapaxes(w_in, 1, 2)
    w_gr_t = jnp.pad(w_in[:, :, GATE_LO:GATE_HI].swapaxes(1, 2), ((0, 0), (0, pad), (0, 0))).astype(BF16)
    gate_up = jnp.pad(gla_gate_up, ((0, 0), (0, pad), (0, 0))).astype(BF16)
    row3 = lambda a: a.reshape(depth, 1, -1)
    attn_g, gate_b, mlp_g = row3(attn_norm), row3(gla_gate_bias), row3(mlp_norm)
    gla_g, ret_g, ret_b = row3(gla_out_norm), row3(ret_norm_gain), row3(ret_norm_bias)
    final_g = final_norm.reshape(1, -1)

    w_t = w_in_t[0].astype(BF16)
    for l in range(depth):
        last = l == depth - 1
        proj, loga, w_out_b = _inproj(xf, attn_g, w_t, w_gr_t, gate_up, gate_b, cos, sin, w_out, l)
        outs = _mixers_outproj(proj, loga, gla_g, ret_g, ret_b, xf, w_out_b, w_up, w_down,
                               None if last else (w_in_t, l + 1), l, seq)
        xf, w_up_b, w_down_b = outs[:3]
        if not last:
            w_t = outs[3]
        xf = _mlp(xf, mlp_g, w_up_b, w_down_b, final_g, l, apply_final_norm=last)
    return xf.reshape(batch, seq, d_model)
```

```python
import functools
import math

import jax
import jax.numpy as jnp
import numpy as np
from jax import lax
from jax.experimental import pallas as pl
from jax.experimental.pallas import tpu as pltpu

D_MODEL = 2048
GLA_HEADS = 4
GLA_DK = 128
GLA_DV = 256
GLA_GATE_RANK = 16
GLA_TAU = 16.0
GLA_CHUNK = 64
RET_HEADS = 4
RET_DK = 256
RET_DV = 256
ROPE_BASE = 10000.0
D_FF = 4 * D_MODEL
EPS = 1e-6

GLA_QK = GLA_HEADS * GLA_DK
GLA_V = GLA_HEADS * GLA_DV
RET_QK = RET_HEADS * RET_DK
RET_V = RET_HEADS * RET_DV
D_PROJ = 2 * GLA_QK + 2 * GLA_V + 2 * RET_QK + 2 * RET_V

OFF_GQ = 0
OFF_GK = OFF_GQ + GLA_QK
OFF_GV = OFF_GK + GLA_QK
OFF_GG = OFF_GV + GLA_V
OFF_RQ = OFF_GG + GLA_V
OFF_RK = OFF_RQ + RET_QK
OFF_RV = OFF_RK + RET_QK
OFF_RG = OFF_RV + RET_V

LANES = 128
MIX_BLOCK = 256
MLP_FF_TILE = 1024
WIN_SLAB = 128
ROPE_WIN_SLAB = 512
VMEM_LIMIT = 60 * 1024 * 1024

F32 = jnp.float32
BF16 = jnp.bfloat16

_NT = (((1,), (1,)), ((), ()))
_TN = (((0,), (0,)), ((), ()))


def _params(n_axes):
    return pltpu.CompilerParams(
        dimension_semantics=("arbitrary",) * n_axes, vmem_limit_bytes=VMEM_LIMIT)


def _rms(x, gain):
    return x * lax.rsqrt(jnp.mean(x * x, axis=-1, keepdims=True) + EPS) * gain


def _win_cast_specs(w_in_t, layer, slab, n_steps):
    d_in = w_in_t.shape[1]
    full_slabs, tail_rows = divmod(d_in, slab)
    n_slabs = full_slabs + (tail_rows > 0)
    last = d_in - slab
    assert last % GLA_GATE_RANK == 0 and tail_rows % GLA_GATE_RANK == 0 and n_slabs <= n_steps
    row = lambda t: pl.multiple_of(jnp.minimum(t * slab, last), GLA_GATE_RANK)
    in_spec = pl.BlockSpec((pl.Element(1), pl.Element(slab), pl.Element(D_MODEL)), lambda t: (layer, row(t), 0))
    out_spec = pl.BlockSpec((slab, D_MODEL), lambda t: (jnp.minimum(t, n_slabs - 1), 0))
    out_shape = jax.ShapeDtypeStruct((n_slabs * slab, D_MODEL), BF16)
    return in_spec, out_spec, out_shape, full_slabs, tail_rows


def _win_cast_step(t, win_ref, win_b_ref, full_slabs, tail_rows):
    slab = win_b_ref.shape[0]

    @pl.when(t < full_slabs)
    def _():
        win_b_ref[...] = win_ref[0].astype(win_b_ref.dtype)

    @pl.when(t >= full_slabs)
    def _():
        win_b_ref[...] = jnp.zeros_like(win_b_ref)
        win_b_ref[:tail_rows, :] = win_ref[0, slab - tail_rows:, :].astype(win_b_ref.dtype)


def _rope_kernel(pos_ref, invf_ref, win_ref, cos_ref, sin_ref, win_b_ref, *, win_full_slabs, win_tail_rows):
    _win_cast_step(pl.program_id(0), win_ref, win_b_ref, win_full_slabs, win_tail_rows)
    rows = pos_ref.shape[0]
    invf = invf_ref[...]
    for r in range(rows):
        p = pos_ref[r:r + 1, :].astype(F32)
        col = jnp.broadcast_to(p, (LANES, LANES)).T
        ang = col * invf
        cos_ref[r * LANES:(r + 1) * LANES, :] = jnp.cos(ang)
        sin_ref[r * LANES:(r + 1) * LANES, :] = jnp.sin(ang)


def _rope_tables(positions, w_in_t):
    n_tok = positions.size
    rows = n_tok // LANES
    rb = 8
    pos2d = positions.reshape(rows, LANES)
    inv_freq = (ROPE_BASE ** (-jnp.arange(0, RET_DK, 2, dtype=F32) / RET_DK)).reshape(1, RET_DK // 2)
    out = jax.ShapeDtypeStruct((n_tok, RET_DK // 2), F32)
    n_steps = rows // rb
    win_in, win_out, win_shape, full_slabs, tail_rows = _win_cast_specs(w_in_t, 0, ROPE_WIN_SLAB, n_steps)
    return pl.pallas_call(
        functools.partial(_rope_kernel, win_full_slabs=full_slabs, win_tail_rows=tail_rows),
        grid=(n_steps,),
        in_specs=[pl.BlockSpec((rb, LANES), lambda i: (i, 0)),
                  pl.BlockSpec((1, RET_DK // 2), lambda i: (0, 0)),
                  win_in],
        out_specs=[pl.BlockSpec((rb * LANES, RET_DK // 2), lambda i: (i, 0))] * 2 + [win_out],
        out_shape=[out, out, win_shape],
        compiler_params=_params(1),
        name="rope_tables",
    )(pos2d, inv_freq, w_in_t)


GATE_LO = 2 * GLA_QK + 2 * GLA_V
GATE_HI = GATE_LO + GLA_GATE_RANK


def _inproj_kernel(x_ref, g_ref, w_ref, wgr_ref, gup_ref, gb_ref, cos_ref, sin_ref, wout_ref,
                   proj_ref, loga_ref, wout_b_ref, h_ref, *, q_tile, k_tile):
    j = pl.program_id(1)

    def project(h):
        return lax.dot_general(h, w_ref[...], _NT, preferred_element_type=F32)

    @pl.when(j == 0)
    def _():
        wout_b_ref[...] = wout_ref[...].astype(wout_b_ref.dtype)
        hb = _rms(x_ref[...], g_ref[...]).astype(BF16)
        h_ref[...] = hb
        gr = lax.dot_general(hb, wgr_ref[...], _NT, preferred_element_type=F32)
        z = jnp.dot(gr.astype(BF16), gup_ref[...], preferred_element_type=F32) + gb_ref[...]
        log_sig = jnp.minimum(z, 0.0) - jnp.log(1.0 + jnp.exp(-jnp.abs(z)))
        loga_ref[...] = log_sig * (1.0 / GLA_TAU)
        proj_ref[...] = project(hb).astype(proj_ref.dtype)

    is_rotary = (j == q_tile) | (j == k_tile)

    @pl.when((j != 0) & jnp.logical_not(is_rotary))
    def _():
        proj_ref[...] = project(h_ref[...]).astype(proj_ref.dtype)

    @pl.when(is_rotary)
    def _():
        p = project(h_ref[...])
        scale = jnp.where(j == k_tile, RET_DK ** -0.5, 1.0).astype(F32)
        cos = cos_ref[...] * scale
        sin = sin_ref[...] * scale
        half = RET_DK // 2
        for h in range(RET_HEADS):
            lo = slice(h * RET_DK, h * RET_DK + half)
            hi = slice(h * RET_DK + half, (h + 1) * RET_DK)
            t1, t2 = p[:, lo], p[:, hi]
            proj_ref[:, lo] = (t1 * cos - t2 * sin).astype(proj_ref.dtype)
            proj_ref[:, hi] = (t2 * cos + t1 * sin).astype(proj_ref.dtype)


def _inproj(x, gain, w_t, w_gr_t, gate_up, gate_bias, cos, sin, w_out, layer, tm=1024):
    n_tok = x.shape[0]
    wout_rows = D_MODEL // (n_tok // tm)
    tn = RET_QK
    assert OFF_RQ % tn == 0 and OFF_RK % tn == 0 and GATE_LO % tn == 0 and OFF_RQ >= tn

    def w_rows(i, j):
        start = j * tn
        row = jnp.where(start < GATE_LO, start, start + GLA_GATE_RANK)
        return pl.multiple_of(row, GLA_GATE_RANK), 0

    return pl.pallas_call(
        functools.partial(_inproj_kernel, q_tile=OFF_RQ // tn, k_tile=OFF_RK // tn),
        grid=(n_tok // tm, D_PROJ // tn),
        in_specs=[pl.BlockSpec((tm, D_MODEL), lambda i, j: (i, 0)),
                  pl.BlockSpec((None, 1, D_MODEL), lambda i, j: (layer, 0, 0)),
                  pl.BlockSpec((pl.Element(tn), pl.Element(D_MODEL)), w_rows),
                  pl.BlockSpec((None, LANES, D_MODEL), lambda i, j: (layer, 0, 0)),
                  pl.BlockSpec((None, LANES, GLA_QK), lambda i, j: (layer, 0, 0)),
                  pl.BlockSpec((None, 1, GLA_QK), lambda i, j: (layer, 0, 0)),
                  pl.BlockSpec((tm, RET_DK // 2), lambda i, j: (i, 0)),
                  pl.BlockSpec((tm, RET_DK // 2), lambda i, j: (i, 0)),
                  pl.BlockSpec((None, wout_rows, D_MODEL), lambda i, j: (layer, i, 0))],
        out_specs=[pl.BlockSpec((tm, tn), lambda i, j: (i, j)),
                   pl.BlockSpec((tm, GLA_QK), lambda i, j: (i, 0)),
                   pl.BlockSpec((wout_rows, D_MODEL), lambda i, j: (i, 0))],
        out_shape=[jax.ShapeDtypeStruct((n_tok, D_PROJ), BF16),
                   jax.ShapeDtypeStruct((n_tok, GLA_QK), F32),
                   jax.ShapeDtypeStruct((D_MODEL, D_MODEL), BF16)],
        scratch_shapes=[pltpu.VMEM((tm, D_MODEL), BF16)],
        compiler_params=_params(2),
        name="norm_inproj",
    )(x, gain, w_t, w_gr_t, gate_up, gate_bias, cos, sin, w_out)


def _split3(x):
    hi = x.astype(BF16)
    r1 = x - hi.astype(F32)
    mid = r1.astype(BF16)
    lo = (r1 - mid.astype(F32)).astype(BF16)
    return hi, mid, lo


def _ret_gamma(h):
    return 1.0 - 2.0 ** (-5.0 - h)


def _ret_decay_tables(t_blk):
    assert t_blk == RET_DK == RET_DV
    log_gamma = np.log1p(-(2.0 ** (-5.0 - np.arange(RET_HEADS))))[:, None, None]
    idx = np.arange(t_blk, dtype=np.float64)
    rel = idx[:, None] - idx[None, :]
    decay = np.where(rel >= 0.0, np.exp(log_gamma * np.maximum(rel, 0.0)), 0.0)
    tail = np.broadcast_to(np.exp(log_gamma * (t_blk - 1.0 - idx[None, :, None])), decay.shape)
    inner = np.broadcast_to(np.exp(log_gamma * (idx[None, :, None] + 1.0)), decay.shape)
    return jnp.asarray(np.stack([decay, tail, inner], axis=1), dtype=F32)


def _silu(g):
    half_g = 0.5 * g
    return half_g + half_g * jnp.tanh(half_g)


def _mix_block(proj_ref, loga_ref, gno_ref, rgain_ref, rbias_ref, dec_ref, mix_ref, sg_ref, sr_ref):
    t_blk = MIX_BLOCK
    n_chunk = t_blk // GLA_CHUNK

    row = lax.broadcasted_iota(jnp.int32, (t_blk, t_blk), 0)
    col = lax.broadcasted_iota(jnp.int32, (t_blk, t_blk), 1)
    causal = col <= row
    chunk_shift = int(math.log2(GLA_CHUNK))
    gla_mask = causal & ((row >> chunk_shift) == (col >> chunk_shift))

    cum = gla_mask.astype(BF16)
    la_hi, la_mid, la_lo = _split3(loga_ref[...])
    b_all = (jnp.dot(cum, la_hi, preferred_element_type=F32)
             + jnp.dot(cum, la_mid, preferred_element_type=F32)
             + jnp.dot(cum, la_lo, preferred_element_type=F32))

    for h in range(GLA_HEADS):
        q = proj_ref[:, OFF_GQ + h * GLA_DK:OFF_GQ + (h + 1) * GLA_DK].astype(F32)
        k = proj_ref[:, OFF_GK + h * GLA_DK:OFF_GK + (h + 1) * GLA_DK].astype(F32)
        v = proj_ref[:, OFF_GV + h * GLA_DV:OFF_GV + (h + 1) * GLA_DV]
        g = proj_ref[:, OFF_GG + h * GLA_DV:OFF_GG + (h + 1) * GLA_DV].astype(F32)
        b = b_all[:, h * GLA_DK:(h + 1) * GLA_DK]
        b_last = [b[(c + 1) * GLA_CHUNK - 1:(c + 1) * GLA_CHUNK, :] for c in range(n_chunk)]
        b_last_rows = jnp.concatenate(
            [jnp.broadcast_to(bl, (GLA_CHUNK, GLA_DK)) for bl in b_last], axis=0)
        q_dec = (q * (GLA_DK ** -0.5) * jnp.exp(b)).astype(BF16)
        k_dec = (k * jnp.exp(-b)).astype(BF16)
        k_tail = (k * jnp.exp(b_last_rows - b)).astype(BF16)

        s = lax.dot_general(q_dec, k_dec, _NT, preferred_element_type=F32)
        s = jnp.where(gla_mask, s, 0.0).astype(BF16)
        o_intra = jnp.dot(s, v, preferred_element_type=F32)

        state = sg_ref[h]
        o_parts = []
        for c in range(n_chunk):
            sl = slice(c * GLA_CHUNK, (c + 1) * GLA_CHUNK)
            o_parts.append(o_intra[sl] + jnp.dot(q_dec[sl], state.astype(BF16),
                                                 preferred_element_type=F32))
            local = lax.dot_general(k_tail[sl], v[sl], _TN, preferred_element_type=F32)
            dcol = jnp.broadcast_to(jnp.exp(b_last[c]), (GLA_DK, GLA_DK)).T
            state = jnp.concatenate([dcol, dcol], axis=1) * state + local
        sg_ref[h] = state

        o = jnp.concatenate(o_parts, axis=0)
        y = o * lax.rsqrt(jnp.mean(o * o, axis=-1, keepdims=True) + EPS)
        y = y * gno_ref[:, h * GLA_DV:(h + 1) * GLA_DV]
        y = y * _silu(g)
        mix_ref[:, h * GLA_DV:(h + 1) * GLA_DV] = y.astype(mix_ref.dtype)

    for h in range(RET_HEADS):
        qb = proj_ref[:, OFF_RQ + h * RET_DK:OFF_RQ + (h + 1) * RET_DK]
        kb = proj_ref[:, OFF_RK + h * RET_DK:OFF_RK + (h + 1) * RET_DK]
        v = proj_ref[:, OFF_RV + h * RET_DV:OFF_RV + (h + 1) * RET_DV]
        g = proj_ref[:, OFF_RG + h * RET_DV:OFF_RG + (h + 1) * RET_DV].astype(F32)
        s = lax.dot_general(qb, kb, _NT, preferred_element_type=F32) * dec_ref[h, 0]
        o_intra = jnp.dot(s.astype(BF16), v, preferred_element_type=F32)
        k_tail = (kb.astype(F32) * dec_ref[h, 1]).astype(BF16)
        local = lax.dot_general(k_tail, v, _TN, preferred_element_type=F32)
        state = sr_ref[h]
        o_inter = jnp.dot(qb, state.astype(BF16), preferred_element_type=F32)
        o = o_intra + o_inter * dec_ref[h, 2]
        sr_ref[h] = _ret_gamma(h) ** t_blk * state + local

        mu = jnp.mean(o, axis=-1, keepdims=True)
        cen = o - mu
        y = cen * lax.rsqrt(jnp.mean(cen * cen, axis=-1, keepdims=True) + EPS)
        y = y * rgain_ref[:, h * RET_DV:(h + 1) * RET_DV] + rbias_ref[:, h * RET_DV:(h + 1) * RET_DV]
        y = y * _silu(g)
        mix_ref[:, GLA_V + h * RET_DV:GLA_V + (h + 1) * RET_DV] = y.astype(mix_ref.dtype)


def _mixer_outproj_kernel(proj_ref, loga_ref, gno_ref, rgain_ref, rbias_ref, dec_ref, x_ref, wout_ref,
                          wup_ref, wdown_ref, *rest, blocks_per_seq, cast_next_win,
                          win_full_slabs=0, win_tail_rows=0):
    if cast_next_win:
        win_ref, out_ref, wup_b_ref, wdown_b_ref, win_b_ref, sg_ref, sr_ref, mix_a, mix_b = rest
    else:
        out_ref, wup_b_ref, wdown_b_ref, sg_ref, sr_ref, mix_a, mix_b = rest
    t = pl.program_id(0)

    @pl.when(lax.rem(t, blocks_per_seq) == 0)
    def _():
        sg_ref[...] = jnp.zeros_like(sg_ref)
        sr_ref[...] = jnp.zeros_like(sr_ref)

    @pl.when(t == 0)
    def _():
        mix_b[...] = jnp.zeros_like(mix_b)

    if cast_next_win:
        _win_cast_step(t, win_ref, win_b_ref, win_full_slabs, win_tail_rows)

    def step(src, dst):
        for k in range(wup_b_ref.shape[0]):
            cols = slice(k * MLP_FF_TILE, (k + 1) * MLP_FF_TILE)
            wup_b_ref[k] = wup_ref[:, cols].astype(wup_b_ref.dtype)
        wdown_b_ref[...] = wdown_ref[...].astype(wdown_b_ref.dtype)
        _mix_block(proj_ref, loga_ref, gno_ref, rgain_ref, rbias_ref, dec_ref, dst, sg_ref, sr_ref)
        out_ref[...] = x_ref[...] + jnp.dot(src[...], wout_ref[...], preferred_element_type=F32)

    n_blk = pl.num_programs(0) - 1
    parity = lax.rem(t, 2)

    @pl.when((parity == 0) & (t < n_blk))
    def _():
        step(mix_b, mix_a)

    @pl.when(parity == 1)
    def _():
        step(mix_a, mix_b)

    @pl.when(t == n_blk)
    def _():
        out_ref[...] = x_ref[...] + jnp.dot(mix_b[...], wout_ref[...], preferred_element_type=F32)


def _mixers_outproj(proj, loga, gla_norm, ret_gain, ret_bias, x, w_out, w_up, w_down, next_win_t, layer, seq):
    n_tok = proj.shape[0]
    t_blk = MIX_BLOCK
    n_blk = n_tok // t_blk
    assert n_blk % 2 == 0
    n_ff = D_FF // MLP_FF_TILE
    cur = lambda t: (jnp.minimum(t, n_blk - 1), 0)
    prev = lambda t: (jnp.maximum(t - 1, 0), 0)
    const = lambda t: (layer, 0, 0)
    slab = lambda t: (layer, jnp.minimum(t, n_blk - 1), 0)
    up_rows, down_rows = D_MODEL // n_blk, D_FF // n_blk
    cast_next_win = next_win_t is not None
    static = dict(blocks_per_seq=seq // t_blk, cast_next_win=cast_next_win)

    in_specs = [pl.BlockSpec((t_blk, D_PROJ), cur),
                pl.BlockSpec((t_blk, GLA_QK), cur),
                pl.BlockSpec((None, 1, GLA_V), const),
                pl.BlockSpec((None, 1, RET_V), const),
                pl.BlockSpec((None, 1, RET_V), const),
                pl.BlockSpec((RET_HEADS, 3, t_blk, t_blk), lambda t: (0, 0, 0, 0)),
                pl.BlockSpec((t_blk, D_MODEL), prev),
                pl.BlockSpec((D_MODEL, D_MODEL), lambda t: (0, 0)),
                pl.BlockSpec((None, up_rows, D_FF), slab),
                pl.BlockSpec((None, down_rows, D_MODEL), slab)]
    out_specs = [pl.BlockSpec((t_blk, D_MODEL), prev),
                 pl.BlockSpec((n_ff, up_rows, MLP_FF_TILE), lambda t: (0, jnp.minimum(t, n_blk - 1), 0)),
                 pl.BlockSpec((down_rows, D_MODEL), cur)]
    out_shape = [jax.ShapeDtypeStruct((n_tok, D_MODEL), F32),
                 jax.ShapeDtypeStruct((n_ff, D_MODEL, MLP_FF_TILE), BF16),
                 jax.ShapeDtypeStruct((D_FF, D_MODEL), BF16)]
    operands = [proj, loga, gla_norm, ret_gain, ret_bias, _ret_decay_tables(t_blk), x, w_out, w_up, w_down]
    if cast_next_win:
        w_in_t, next_layer = next_win_t
        win_in, win_out, win_shape, full_slabs, tail_rows = _win_cast_specs(w_in_t, next_layer, WIN_SLAB, n_blk)
        in_specs.append(win_in)
        out_specs.append(win_out)
        out_shape.append(win_shape)
        static.update(win_full_slabs=full_slabs, win_tail_rows=tail_rows)
        operands.append(w_in_t)

    return pl.pallas_call(
        functools.partial(_mixer_outproj_kernel, **static),
        grid=(n_blk + 1,),
        in_specs=in_specs,
        out_specs=out_specs,
        out_shape=out_shape,
        scratch_shapes=[pltpu.VMEM((GLA_HEADS, GLA_DK, GLA_DV), F32),
                        pltpu.VMEM((RET_HEADS, RET_DK, RET_DV), F32),
                        pltpu.VMEM((t_blk, D_MODEL), BF16),
                        pltpu.VMEM((t_blk, D_MODEL), BF16)],
        compiler_params=_params(1),
        name="mixers_outproj",
    )(*operands)


def _mlp_kernel(x_ref, g_ref, wu_ref, wd_ref, fg_ref, o_ref, h_ref, *, apply_final_norm):
    k = pl.program_id(1)

    def ff_tile(h):
        u = jnp.dot(h, wu_ref[...], preferred_element_type=F32)
        a = jnp.square(jnp.maximum(u, 0.0)).astype(BF16)
        return jnp.dot(a, wd_ref[...], preferred_element_type=F32)

    @pl.when(k == 0)
    def _():
        x = x_ref[...]
        h = _rms(x, g_ref[...]).astype(BF16)
        h_ref[...] = h
        o_ref[...] = x + ff_tile(h)

    @pl.when(k != 0)
    def _():
        o_ref[...] += ff_tile(h_ref[...])

    if apply_final_norm:
        @pl.when(k == pl.num_programs(1) - 1)
        def _():
            o_ref[...] = _rms(o_ref[...], fg_ref[...])


def _mlp(x, gain, w_up, w_down, final_gain, layer, apply_final_norm, tm=1024):
    n_tok = x.shape[0]
    tf = MLP_FF_TILE
    return pl.pallas_call(
        functools.partial(_mlp_kernel, apply_final_norm=apply_final_norm),
        grid=(n_tok // tm, D_FF // tf),
        in_specs=[pl.BlockSpec((tm, D_MODEL), lambda i, k: (i, 0)),
                  pl.BlockSpec((None, 1, D_MODEL), lambda i, k: (layer, 0, 0)),
                  pl.BlockSpec((None, D_MODEL, tf), lambda i, k: (k, 0, 0)),
                  pl.BlockSpec((tf, D_MODEL), lambda i, k: (k, 0)),
                  pl.BlockSpec((1, D_MODEL), lambda i, k: (0, 0))],
        out_specs=pl.BlockSpec((tm, D_MODEL), lambda i, k: (i, 0)),
        out_shape=jax.ShapeDtypeStruct((n_tok, D_MODEL), F32),
        scratch_shapes=[pltpu.VMEM((tm, D_MODEL), BF16)],
        compiler_params=_params(2),
        name="norm_mlp_residual",
    )(x, gain, w_up, w_down, final_gain)


def kernel(x, positions, attn_norm, w_in, gla_gate_up, gla_gate_bias, gla_out_norm,
           ret_norm_gain, ret_norm_bias, w_out, mlp_norm, w_up, w_down, final_norm):
    batch, seq, d_model = x.shape
    depth = w_in.shape[0]
    assert d_model == D_MODEL and seq % MIX_BLOCK == 0
    n_tok = batch * seq
    xf = x.reshape(n_tok, d_model)

    pad = LANES - GLA_GATE_RANK
    w_in_t = jnp.swapaxes(w_in, 1, 2)
    cos, sin, w_t = _rope_tables(positions, w_in_t)
    w_gr_t = jnp.pad(w_in[:, :, GATE_LO:GATE_HI].swapaxes(1, 2), ((0, 0), (0, pad), (0, 0))).astype(BF16)
    gate_up = jnp.pad(gla_gate_up, ((0, 0), (0, pad), (0, 0))).astype(BF16)
    row3 = lambda a: a.reshape(depth, 1, -1)
    attn_g, gate_b, mlp_g = row3(attn_norm), row3(gla_gate_bias), row3(mlp_norm)
    gla_g, ret_g, ret_b = row3(gla_out_norm), row3(ret_norm_gain), row3(ret_norm_bias)
    final_g = final_norm.reshape(1, -1)

    for l in range(depth):
        last = l == depth - 1
        proj, loga, w_out_b = _inproj(xf, attn_g, w_t, w_gr_t, gate_up, gate_b, cos, sin, w_out, l)
        outs = _mixers_outproj(proj, loga, gla_g, ret_g, ret_b, xf, w_out_b, w_up, w_down,
                               None if last else (w_in_t, l + 1), l, seq)
        xf, w_up_b, w_down_b = outs[:3]
        if not last:
            w_t = outs[3]
        xf = _mlp(xf, mlp_g, w_up_b, w_down_b, final_g, l, apply_final_norm=last)
    return xf.reshape(batch, seq, d_model)
```

```python
import functools
import math

import jax
import jax.numpy as jnp
import numpy as np
from jax import lax
from jax.experimental import pallas as pl
from jax.experimental.pallas import tpu as pltpu

D_MODEL = 2048
GLA_HEADS = 4
GLA_DK = 128
GLA_DV = 256
GLA_GATE_RANK = 16
GLA_TAU = 16.0
GLA_CHUNK = 64
RET_HEADS = 4
RET_DK = 256
RET_DV = 256
ROPE_BASE = 10000.0
D_FF = 4 * D_MODEL
EPS = 1e-6

GLA_QK = GLA_HEADS * GLA_DK
GLA_V = GLA_HEADS * GLA_DV
RET_QK = RET_HEADS * RET_DK
RET_V = RET_HEADS * RET_DV
D_PROJ = 2 * GLA_QK + 2 * GLA_V + 2 * RET_QK + 2 * RET_V

OFF_GQ = 0
OFF_GK = OFF_GQ + GLA_QK
OFF_GV = OFF_GK + GLA_QK
OFF_GG = OFF_GV + GLA_V
OFF_RQ = OFF_GG + GLA_V
OFF_RK = OFF_RQ + RET_QK
OFF_RV = OFF_RK + RET_QK
OFF_RG = OFF_RV + RET_V

LANES = 128
MIX_BLOCK = 256
MIX_STEP_BLOCKS = 1
MLP_FF_TILE = 1024
WIN_SLAB = 128
ROPE_WIN_SLAB = 512
VMEM_LIMIT = 60 * 1024 * 1024

F32 = jnp.float32
BF16 = jnp.bfloat16

_NT = (((1,), (1,)), ((), ()))
_TN = (((0,), (0,)), ((), ()))


def _params(n_axes):
    return pltpu.CompilerParams(
        dimension_semantics=("arbitrary",) * n_axes, vmem_limit_bytes=VMEM_LIMIT)


def _rms(x, gain):
    return x * lax.rsqrt(jnp.mean(x * x, axis=-1, keepdims=True) + EPS) * gain


def _win_cast_specs(w_in_t, layer, slab, n_steps):
    d_in = w_in_t.shape[1]
    full_slabs, tail_rows = divmod(d_in, slab)
    n_slabs = full_slabs + (tail_rows > 0)
    last = d_in - slab
    assert last % GLA_GATE_RANK == 0 and tail_rows % GLA_GATE_RANK == 0 and n_slabs <= n_steps
    row = lambda t: pl.multiple_of(jnp.minimum(t * slab, last), GLA_GATE_RANK)
    in_spec = pl.BlockSpec((pl.Element(1), pl.Element(slab), pl.Element(D_MODEL)), lambda t: (layer, row(t), 0))
    out_spec = pl.BlockSpec((slab, D_MODEL), lambda t: (jnp.minimum(t, n_slabs - 1), 0))
    out_shape = jax.ShapeDtypeStruct((n_slabs * slab, D_MODEL), BF16)
    return in_spec, out_spec, out_shape, full_slabs, tail_rows


def _win_cast_step(t, win_ref, win_b_ref, full_slabs, tail_rows):
    slab = win_b_ref.shape[0]

    @pl.when(t < full_slabs)
    def _():
        win_b_ref[...] = win_ref[0].astype(win_b_ref.dtype)

    @pl.when(t >= full_slabs)
    def _():
        win_b_ref[...] = jnp.zeros_like(win_b_ref)
        win_b_ref[:tail_rows, :] = win_ref[0, slab - tail_rows:, :].astype(win_b_ref.dtype)


def _rope_kernel(pos_ref, invf_ref, win_ref, cos_ref, sin_ref, win_b_ref, *, win_full_slabs, win_tail_rows):
    _win_cast_step(pl.program_id(0), win_ref, win_b_ref, win_full_slabs, win_tail_rows)
    rows = pos_ref.shape[0]
    invf = invf_ref[...]
    for r in range(rows):
        p = pos_ref[r:r + 1, :].astype(F32)
        col = jnp.broadcast_to(p, (LANES, LANES)).T
        ang = col * invf
        cos_ref[r * LANES:(r + 1) * LANES, :] = jnp.cos(ang)
        sin_ref[r * LANES:(r + 1) * LANES, :] = jnp.sin(ang)


def _rope_tables(positions, w_in_t):
    n_tok = positions.size
    rows = n_tok // LANES
    rb = 8
    pos2d = positions.reshape(rows, LANES)
    inv_freq = (ROPE_BASE ** (-jnp.arange(0, RET_DK, 2, dtype=F32) / RET_DK)).reshape(1, RET_DK // 2)
    out = jax.ShapeDtypeStruct((n_tok, RET_DK // 2), F32)
    n_steps = rows // rb
    win_in, win_out, win_shape, full_slabs, tail_rows = _win_cast_specs(w_in_t, 0, ROPE_WIN_SLAB, n_steps)
    return pl.pallas_call(
        functools.partial(_rope_kernel, win_full_slabs=full_slabs, win_tail_rows=tail_rows),
        grid=(n_steps,),
        in_specs=[pl.BlockSpec((rb, LANES), lambda i: (i, 0)),
                  pl.BlockSpec((1, RET_DK // 2), lambda i: (0, 0)),
                  win_in],
        out_specs=[pl.BlockSpec((rb * LANES, RET_DK // 2), lambda i: (i, 0))] * 2 + [win_out],
        out_shape=[out, out, win_shape],
        compiler_params=_params(1),
        name="rope_tables",
    )(pos2d, inv_freq, w_in_t)


GATE_LO = 2 * GLA_QK + 2 * GLA_V
GATE_HI = GATE_LO + GLA_GATE_RANK


def _inproj_kernel(x_ref, g_ref, w_ref, wgr_ref, gup_ref, gb_ref, cos_ref, sin_ref, wout_ref,
                   proj_ref, loga_ref, wout_b_ref, h_ref, *, q_tile, k_tile):
    j = pl.program_id(1)

    def project(h):
        return lax.dot_general(h, w_ref[...], _NT, preferred_element_type=F32)

    @pl.when(j == 0)
    def _():
        wout_b_ref[...] = wout_ref[...].astype(wout_b_ref.dtype)
        hb = _rms(x_ref[...], g_ref[...]).astype(BF16)
        h_ref[...] = hb
        gr = lax.dot_general(hb, wgr_ref[...], _NT, preferred_element_type=F32)
        z = jnp.dot(gr.astype(BF16), gup_ref[...], preferred_element_type=F32) + gb_ref[...]
        log_sig = jnp.minimum(z, 0.0) - jnp.log(1.0 + jnp.exp(-jnp.abs(z)))
        loga_ref[...] = log_sig * (1.0 / GLA_TAU)
        proj_ref[...] = project(hb).astype(proj_ref.dtype)

    is_rotary = (j == q_tile) | (j == k_tile)

    @pl.when((j != 0) & jnp.logical_not(is_rotary))
    def _():
        proj_ref[...] = project(h_ref[...]).astype(proj_ref.dtype)

    @pl.when(is_rotary)
    def _():
        p = project(h_ref[...])
        scale = jnp.where(j == k_tile, RET_DK ** -0.5, 1.0).astype(F32)
        cos = cos_ref[...] * scale
        sin = sin_ref[...] * scale
        half = RET_DK // 2
        for h in range(RET_HEADS):
            lo = slice(h * RET_DK, h * RET_DK + half)
            hi = slice(h * RET_DK + half, (h + 1) * RET_DK)
            t1, t2 = p[:, lo], p[:, hi]
            proj_ref[:, lo] = (t1 * cos - t2 * sin).astype(proj_ref.dtype)
            proj_ref[:, hi] = (t2 * cos + t1 * sin).astype(proj_ref.dtype)


def _inproj(x, gain, w_t, w_gr_t, gate_up, gate_bias, cos, sin, w_out, layer, tm=1024):
    n_tok = x.shape[0]
    wout_rows = D_MODEL // (n_tok // tm)
    tn = RET_QK
    assert OFF_RQ % tn == 0 and OFF_RK % tn == 0 and GATE_LO % tn == 0 and OFF_RQ >= tn

    def w_rows(i, j):
        start = j * tn
        row = jnp.where(start < GATE_LO, start, start + GLA_GATE_RANK)
        return pl.multiple_of(row, GLA_GATE_RANK), 0

    return pl.pallas_call(
        functools.partial(_inproj_kernel, q_tile=OFF_RQ // tn, k_tile=OFF_RK // tn),
        grid=(n_tok // tm, D_PROJ // tn),
        in_specs=[pl.BlockSpec((tm, D_MODEL), lambda i, j: (i, 0)),
                  pl.BlockSpec((None, 1, D_MODEL), lambda i, j: (layer, 0, 0)),
                  pl.BlockSpec((pl.Element(tn), pl.Element(D_MODEL)), w_rows),
                  pl.BlockSpec((None, LANES, D_MODEL), lambda i, j: (layer, 0, 0)),
                  pl.BlockSpec((None, LANES, GLA_QK), lambda i, j: (layer, 0, 0)),
                  pl.BlockSpec((None, 1, GLA_QK), lambda i, j: (layer, 0, 0)),
                  pl.BlockSpec((tm, RET_DK // 2), lambda i, j: (i, 0)),
                  pl.BlockSpec((tm, RET_DK // 2), lambda i, j: (i, 0)),
                  pl.BlockSpec((None, wout_rows, D_MODEL), lambda i, j: (layer, i, 0))],
        out_specs=[pl.BlockSpec((tm, tn), lambda i, j: (i, j)),
                   pl.BlockSpec((tm, GLA_QK), lambda i, j: (i, 0)),
                   pl.BlockSpec((wout_rows, D_MODEL), lambda i, j: (i, 0))],
        out_shape=[jax.ShapeDtypeStruct((n_tok, D_PROJ), BF16),
                   jax.ShapeDtypeStruct((n_tok, GLA_QK), F32),
                   jax.ShapeDtypeStruct((D_MODEL, D_MODEL), BF16)],
        scratch_shapes=[pltpu.VMEM((tm, D_MODEL), BF16)],
        compiler_params=_params(2),
        name="norm_inproj",
    )(x, gain, w_t, w_gr_t, gate_up, gate_bias, cos, sin, w_out)


def _split3(x):
    hi = x.astype(BF16)
    r1 = x - hi.astype(F32)
    mid = r1.astype(BF16)
    lo = (r1 - mid.astype(F32)).astype(BF16)
    return hi, mid, lo


def _ret_gamma(h):
    return 1.0 - 2.0 ** (-5.0 - h)


def _ret_decay_tables(t_blk):
    assert t_blk == RET_DK == RET_DV
    log_gamma = np.log1p(-(2.0 ** (-5.0 - np.arange(RET_HEADS))))[:, None, None]
    idx = np.arange(t_blk, dtype=np.float64)
    rel = idx[:, None] - idx[None, :]
    decay = np.where(rel >= 0.0, np.exp(log_gamma * np.maximum(rel, 0.0)), 0.0)
    tail = np.broadcast_to(np.exp(log_gamma * (t_blk - 1.0 - idx[None, :, None])), decay.shape)
    inner = np.broadcast_to(np.exp(log_gamma * (idx[None, :, None] + 1.0)), decay.shape)
    return jnp.asarray(np.stack([decay, tail, inner], axis=1), dtype=F32)


def _silu(g):
    half_g = 0.5 * g
    return half_g + half_g * jnp.tanh(half_g)


def _mix_block(proj_ref, loga_ref, gno_ref, rgain_ref, rbias_ref, dec_ref, mix_ref, sg_ref, sr_ref):
    t_blk = MIX_BLOCK
    n_chunk = t_blk // GLA_CHUNK

    row = lax.broadcasted_iota(jnp.int32, (t_blk, t_blk), 0)
    col = lax.broadcasted_iota(jnp.int32, (t_blk, t_blk), 1)
    causal = col <= row
    chunk_shift = int(math.log2(GLA_CHUNK))
    gla_mask = causal & ((row >> chunk_shift) == (col >> chunk_shift))

    cum = gla_mask.astype(BF16)
    la_hi, la_mid, la_lo = _split3(loga_ref[...])
    b_all = (jnp.dot(cum, la_hi, preferred_element_type=F32)
             + jnp.dot(cum, la_mid, preferred_element_type=F32)
             + jnp.dot(cum, la_lo, preferred_element_type=F32))

    for h in range(GLA_HEADS):
        q = proj_ref[:, OFF_GQ + h * GLA_DK:OFF_GQ + (h + 1) * GLA_DK].astype(F32)
        k = proj_ref[:, OFF_GK + h * GLA_DK:OFF_GK + (h + 1) * GLA_DK].astype(F32)
        v = proj_ref[:, OFF_GV + h * GLA_DV:OFF_GV + (h + 1) * GLA_DV]
        g = proj_ref[:, OFF_GG + h * GLA_DV:OFF_GG + (h + 1) * GLA_DV].astype(F32)
        b = b_all[:, h * GLA_DK:(h + 1) * GLA_DK]
        b_last = [b[(c + 1) * GLA_CHUNK - 1:(c + 1) * GLA_CHUNK, :] for c in range(n_chunk)]
        pre = [jnp.zeros_like(b_last[0])]
        for c in range(1, n_chunk):
            pre.append(pre[-1] + b_last[c - 1])
        post = [jnp.zeros_like(b_last[0])]
        for c in range(n_chunk - 2, -1, -1):
            post.insert(0, post[0] + b_last[c + 1])
        by_rows = lambda vs: jnp.concatenate([jnp.broadcast_to(x, (GLA_CHUNK, GLA_DK)) for x in vs], axis=0)
        chunk = lambda a, c: a[c * GLA_CHUNK:(c + 1) * GLA_CHUNK]

        q_dec32 = q * (GLA_DK ** -0.5) * jnp.exp(b)
        q_dec = q_dec32.astype(BF16)
        q_blk = (q_dec32 * jnp.exp(by_rows(pre))).astype(BF16)
        k_dec = (k * jnp.exp(-b)).astype(BF16)
        k_tail32 = k * jnp.exp(by_rows(b_last) - b)
        k_tail = k_tail32.astype(BF16)
        k_end = (k_tail32 * jnp.exp(by_rows(post))).astype(BF16)

        s_rows = []
        for i in range(n_chunk):
            keys = []
            for j in range(n_chunk):
                if j + 1 == i:
                    keys.append(chunk(k_tail, j))
                elif j < i:
                    keys.append((chunk(k_tail32, j) * jnp.exp(pre[i] - pre[j + 1])).astype(BF16))
                else:
                    keys.append(chunk(k_dec, j))
            s_rows.append(lax.dot_general(chunk(q_dec, i), jnp.concatenate(keys, axis=0), _NT,
                                          preferred_element_type=F32))
        s = jnp.where(causal, jnp.concatenate(s_rows, axis=0), 0.0).astype(BF16)

        state = sg_ref[h]
        o = (jnp.dot(s, v, preferred_element_type=F32)
             + jnp.dot(q_blk, state.astype(BF16), preferred_element_type=F32))
        dcol = jnp.broadcast_to(jnp.exp(pre[-1] + b_last[-1]), (GLA_DK, GLA_DK)).T
        sg_ref[h] = (jnp.concatenate([dcol, dcol], axis=1) * state
                     + lax.dot_general(k_end, v, _TN, preferred_element_type=F32))

        y = o * lax.rsqrt(jnp.mean(o * o, axis=-1, keepdims=True) + EPS)
        y = y * gno_ref[:, h * GLA_DV:(h + 1) * GLA_DV]
        y = y * _silu(g)
        mix_ref[:, h * GLA_DV:(h + 1) * GLA_DV] = y.astype(mix_ref.dtype)

    for h in range(RET_HEADS):
        qb = proj_ref[:, OFF_RQ + h * RET_DK:OFF_RQ + (h + 1) * RET_DK]
        kb = proj_ref[:, OFF_RK + h * RET_DK:OFF_RK + (h + 1) * RET_DK]
        v = proj_ref[:, OFF_RV + h * RET_DV:OFF_RV + (h + 1) * RET_DV]
        g = proj_ref[:, OFF_RG + h * RET_DV:OFF_RG + (h + 1) * RET_DV].astype(F32)
        s = lax.dot_general(qb, kb, _NT, preferred_element_type=F32) * dec_ref[h, 0]
        o_intra = jnp.dot(s.astype(BF16), v, preferred_element_type=F32)
        k_tail = (kb.astype(F32) * dec_ref[h, 1]).astype(BF16)
        local = lax.dot_general(k_tail, v, _TN, preferred_element_type=F32)
        state = sr_ref[h]
        o_inter = jnp.dot(qb, state.astype(BF16), preferred_element_type=F32)
        o = o_intra + o_inter * dec_ref[h, 2]
        sr_ref[h] = _ret_gamma(h) ** t_blk * state + local

        mu = jnp.mean(o, axis=-1, keepdims=True)
        cen = o - mu
        y = cen * lax.rsqrt(jnp.mean(cen * cen, axis=-1, keepdims=True) + EPS)
        y = y * rgain_ref[:, h * RET_DV:(h + 1) * RET_DV] + rbias_ref[:, h * RET_DV:(h + 1) * RET_DV]
        y = y * _silu(g)
        mix_ref[:, GLA_V + h * RET_DV:GLA_V + (h + 1) * RET_DV] = y.astype(mix_ref.dtype)


def _mixer_outproj_kernel(proj_ref, loga_ref, gno_ref, rgain_ref, rbias_ref, dec_ref, x_ref, wout_ref,
                          wup_ref, wdown_ref, *rest, blocks_per_seq, cast_next_win,
                          win_full_slabs=0, win_tail_rows=0):
    if cast_next_win:
        win_ref, out_ref, wup_b_ref, wdown_b_ref, win_b_ref, sg_ref, sr_ref, mix_a, mix_b = rest
    else:
        out_ref, wup_b_ref, wdown_b_ref, sg_ref, sr_ref, mix_a, mix_b = rest
    t = pl.program_id(0)

    @pl.when(lax.rem(t, blocks_per_seq) == 0)
    def _():
        sg_ref[...] = jnp.zeros_like(sg_ref)
        sr_ref[...] = jnp.zeros_like(sr_ref)

    @pl.when(t == 0)
    def _():
        mix_b[...] = jnp.zeros_like(mix_b)

    if cast_next_win:
        _win_cast_step(t, win_ref, win_b_ref, win_full_slabs, win_tail_rows)

    def step(src, dst):
        for k in range(wup_b_ref.shape[0]):
            cols = slice(k * MLP_FF_TILE, (k + 1) * MLP_FF_TILE)
            wup_b_ref[k] = wup_ref[:, cols].astype(wup_b_ref.dtype)
        wdown_b_ref[...] = wdown_ref[...].astype(wdown_b_ref.dtype)
        for blk in range(MIX_STEP_BLOCKS):
            rows = pl.ds(blk * MIX_BLOCK, MIX_BLOCK)
            _mix_block(proj_ref.at[rows], loga_ref.at[rows], gno_ref, rgain_ref, rbias_ref, dec_ref,
                       dst.at[rows], sg_ref, sr_ref)
        out_ref[...] = x_ref[...] + jnp.dot(src[...], wout_ref[...], preferred_element_type=F32)

    n_blk = pl.num_programs(0) - 1
    parity = lax.rem(t, 2)

    @pl.when((parity == 0) & (t < n_blk))
    def _():
        step(mix_b, mix_a)

    @pl.when(parity == 1)
    def _():
        step(mix_a, mix_b)

    @pl.when(t == n_blk)
    def _():
        out_ref[...] = x_ref[...] + jnp.dot(mix_b[...], wout_ref[...], preferred_element_type=F32)


def _mixers_outproj(proj, loga, gla_norm, ret_gain, ret_bias, x, w_out, w_up, w_down, next_win_t, layer, seq):
    n_tok = proj.shape[0]
    t_blk = MIX_BLOCK * MIX_STEP_BLOCKS
    n_blk = n_tok // t_blk
    assert n_blk % 2 == 0 and seq % t_blk == 0
    n_ff = D_FF // MLP_FF_TILE
    cur = lambda t: (jnp.minimum(t, n_blk - 1), 0)
    prev = lambda t: (jnp.maximum(t - 1, 0), 0)
    const = lambda t: (layer, 0, 0)
    slab = lambda t: (layer, jnp.minimum(t, n_blk - 1), 0)
    up_rows, down_rows = D_MODEL // n_blk, D_FF // n_blk
    cast_next_win = next_win_t is not None
    static = dict(blocks_per_seq=seq // t_blk, cast_next_win=cast_next_win)

    in_specs = [pl.BlockSpec((t_blk, D_PROJ), cur),
                pl.BlockSpec((t_blk, GLA_QK), cur),
                pl.BlockSpec((None, 1, GLA_V), const),
                pl.BlockSpec((None, 1, RET_V), const),
                pl.BlockSpec((None, 1, RET_V), const),
                pl.BlockSpec((RET_HEADS, 3, MIX_BLOCK, MIX_BLOCK), lambda t: (0, 0, 0, 0)),
                pl.BlockSpec((t_blk, D_MODEL), prev),
                pl.BlockSpec((D_MODEL, D_MODEL), lambda t: (0, 0)),
                pl.BlockSpec((None, up_rows, D_FF), slab),
                pl.BlockSpec((None, down_rows, D_MODEL), slab)]
    out_specs = [pl.BlockSpec((t_blk, D_MODEL), prev),
                 pl.BlockSpec((n_ff, up_rows, MLP_FF_TILE), lambda t: (0, jnp.minimum(t, n_blk - 1), 0)),
                 pl.BlockSpec((down_rows, D_MODEL), cur)]
    out_shape = [jax.ShapeDtypeStruct((n_tok, D_MODEL), F32),
                 jax.ShapeDtypeStruct((n_ff, D_MODEL, MLP_FF_TILE), BF16),
                 jax.ShapeDtypeStruct((D_FF, D_MODEL), BF16)]
    operands = [proj, loga, gla_norm, ret_gain, ret_bias, _ret_decay_tables(MIX_BLOCK), x, w_out, w_up, w_down]
    if cast_next_win:
        w_in_t, next_layer = next_win_t
        win_in, win_out, win_shape, full_slabs, tail_rows = _win_cast_specs(w_in_t, next_layer, WIN_SLAB, n_blk)
        in_specs.append(win_in)
        out_specs.append(win_out)
        out_shape.append(win_shape)
        static.update(win_full_slabs=full_slabs, win_tail_rows=tail_rows)
        operands.append(w_in_t)

    return pl.pallas_call(
        functools.partial(_mixer_outproj_kernel, **static),
        grid=(n_blk + 1,),
        in_specs=in_specs,
        out_specs=out_specs,
        out_shape=out_shape,
        scratch_shapes=[pltpu.VMEM((GLA_HEADS, GLA_DK, GLA_DV), F32),
                        pltpu.VMEM((RET_HEADS, RET_DK, RET_DV), F32),
                        pltpu.VMEM((t_blk, D_MODEL), BF16),
                        pltpu.VMEM((t_blk, D_MODEL), BF16)],
        compiler_params=_params(1),
        name="mixers_outproj",
    )(*operands)


def _mlp_kernel(x_ref, g_ref, wu_ref, wd_ref, fg_ref, o_ref, h_ref, *, apply_final_norm):
    k = pl.program_id(1)

    def ff_tile(h):
        u = jnp.dot(h, wu_ref[...], preferred_element_type=F32)
        a = jnp.square(jnp.maximum(u, 0.0)).astype(BF16)
        return jnp.dot(a, wd_ref[...], preferred_element_type=F32)

    @pl.when(k == 0)
    def _():
        x = x_ref[...]
        h = _rms(x, g_ref[...]).astype(BF16)
        h_ref[...] = h
        o_ref[...] = x + ff_tile(h)

    @pl.when(k != 0)
    def _():
        o_ref[...] += ff_tile(h_ref[...])

    if apply_final_norm:
        @pl.when(k == pl.num_programs(1) - 1)
        def _():
            o_ref[...] = _rms(o_ref[...], fg_ref[...])


def _mlp(x, gain, w_up, w_down, final_gain, layer, apply_final_norm, tm=1024):
    n_tok = x.shape[0]
    tf = MLP_FF_TILE
    return pl.pallas_call(
        functools.partial(_mlp_kernel, apply_final_norm=apply_final_norm),
        grid=(n_tok // tm, D_FF // tf),
        in_specs=[pl.BlockSpec((tm, D_MODEL), lambda i, k: (i, 0)),
                  pl.BlockSpec((None, 1, D_MODEL), lambda i, k: (layer, 0, 0)),
                  pl.BlockSpec((None, D_MODEL, tf), lambda i, k: (k, 0, 0)),
                  pl.BlockSpec((tf, D_MODEL), lambda i, k: (k, 0)),
                  pl.BlockSpec((1, D_MODEL), lambda i, k: (0, 0))],
        out_specs=pl.BlockSpec((tm, D_MODEL), lambda i, k: (i, 0)),
        out_shape=jax.ShapeDtypeStruct((n_tok, D_MODEL), F32),
        scratch_shapes=[pltpu.VMEM((tm, D_MODEL), BF16)],
        compiler_params=_params(2),
        name="norm_mlp_residual",
    )(x, gain, w_up, w_down, final_gain)


def kernel(x, positions, attn_norm, w_in, gla_gate_up, gla_gate_bias, gla_out_norm,
           ret_norm_gain, ret_norm_bias, w_out, mlp_norm, w_up, w_down, final_norm):
    batch, seq, d_model = x.shape
    depth = w_in.shape[0]
    assert d_model == D_MODEL and seq % (MIX_BLOCK * MIX_STEP_BLOCKS) == 0
    n_tok = batch * seq
    xf = x.reshape(n_tok, d_model)

    pad = LANES - GLA_GATE_RANK
    w_in_t = jnp.swapaxes(w_in, 1, 2)
    cos, sin, w_t = _rope_tables(positions, w_in_t)
    w_gr_t = jnp.pad(w_in[:, :, GATE_LO:GATE_HI].swapaxes(1, 2), ((0, 0), (0, pad), (0, 0))).astype(BF16)
    gate_up = jnp.pad(gla_gate_up, ((0, 0), (0, pad), (0, 0))).astype(BF16)
    row3 = lambda a: a.reshape(depth, 1, -1)
    attn_g, gate_b, mlp_g = row3(attn_norm), row3(gla_gate_bias), row3(mlp_norm)
    gla_g, ret_g, ret_b = row3(gla_out_norm), row3(ret_norm_gain), row3(ret_norm_bias)
    final_g = final_norm.reshape(1, -1)

    for l in range(depth):
        last = l == depth - 1
        proj, loga, w_out_b = _inproj(xf, attn_g, w_t, w_gr_t, gate_up, gate_b, cos, sin, w_out, l)
        outs = _mixers_outproj(proj, loga, gla_g, ret_g, ret_b, xf, w_out_b, w_up, w_down,
                               None if last else (w_in_t, l + 1), l, seq)
        xf, w_up_b, w_down_b = outs[:3]
        if not last:
            w_t = outs[3]
        xf = _mlp(xf, mlp_g, w_up_b, w_down_b, final_g, l, apply_final_norm=last)
    return xf.reshape(batch, seq, d_model)
```

```python
import functools
import math

import jax
import jax.numpy as jnp
import numpy as np
from jax import lax
from jax.experimental import pallas as pl
from jax.experimental.pallas import tpu as pltpu

D_MODEL = 2048
GLA_HEADS = 4
GLA_DK = 128
GLA_DV = 256
GLA_GATE_RANK = 16
GLA_TAU = 16.0
GLA_CHUNK = 64
RET_HEADS = 4
RET_DK = 256
RET_DV = 256
ROPE_BASE = 10000.0
D_FF = 4 * D_MODEL
EPS = 1e-6

GLA_QK = GLA_HEADS * GLA_DK
GLA_V = GLA_HEADS * GLA_DV
RET_QK = RET_HEADS * RET_DK
RET_V = RET_HEADS * RET_DV
D_PROJ = 2 * GLA_QK + 2 * GLA_V + 2 * RET_QK + 2 * RET_V

OFF_GQ = 0
OFF_GK = OFF_GQ + GLA_QK
OFF_GV = OFF_GK + GLA_QK
OFF_GG = OFF_GV + GLA_V
OFF_RQ = OFF_GG + GLA_V
OFF_RK = OFF_RQ + RET_QK
OFF_RV = OFF_RK + RET_QK
OFF_RG = OFF_RV + RET_V

LANES = 128
MIX_BLOCK = 256
MIX_STEP_BLOCKS = 1
MLP_FF_TILE = 1024
WIN_SLAB = 128
ROPE_WIN_SLAB = 512
VMEM_LIMIT = 60 * 1024 * 1024

F32 = jnp.float32
BF16 = jnp.bfloat16

_NT = (((1,), (1,)), ((), ()))
_TN = (((0,), (0,)), ((), ()))


def _params(n_axes):
    return pltpu.CompilerParams(
        dimension_semantics=("arbitrary",) * n_axes, vmem_limit_bytes=VMEM_LIMIT)


def _rms(x, gain):
    return x * lax.rsqrt(jnp.mean(x * x, axis=-1, keepdims=True) + EPS) * gain


def _win_cast_specs(w_in_t, layer, slab, n_steps):
    d_in = w_in_t.shape[1]
    full_slabs, tail_rows = divmod(d_in, slab)
    n_slabs = full_slabs + (tail_rows > 0)
    last = d_in - slab
    assert last % GLA_GATE_RANK == 0 and tail_rows % GLA_GATE_RANK == 0 and n_slabs <= n_steps
    row = lambda t: pl.multiple_of(jnp.minimum(t * slab, last), GLA_GATE_RANK)
    in_spec = pl.BlockSpec((pl.Element(1), pl.Element(slab), pl.Element(D_MODEL)), lambda t: (layer, row(t), 0))
    out_spec = pl.BlockSpec((slab, D_MODEL), lambda t: (jnp.minimum(t, n_slabs - 1), 0))
    out_shape = jax.ShapeDtypeStruct((n_slabs * slab, D_MODEL), BF16)
    return in_spec, out_spec, out_shape, full_slabs, tail_rows


def _win_cast_step(t, win_ref, win_b_ref, full_slabs, tail_rows):
    slab = win_b_ref.shape[0]

    @pl.when(t < full_slabs)
    def _():
        win_b_ref[...] = win_ref[0].astype(win_b_ref.dtype)

    @pl.when(t >= full_slabs)
    def _():
        win_b_ref[...] = jnp.zeros_like(win_b_ref)
        win_b_ref[:tail_rows, :] = win_ref[0, slab - tail_rows:, :].astype(win_b_ref.dtype)


def _rope_kernel(pos_ref, invf_ref, win_ref, cos_ref, sin_ref, win_b_ref, *, win_full_slabs, win_tail_rows):
    _win_cast_step(pl.program_id(0), win_ref, win_b_ref, win_full_slabs, win_tail_rows)
    rows = pos_ref.shape[0]
    invf = invf_ref[...]
    for r in range(rows):
        p = pos_ref[r:r + 1, :].astype(F32)
        col = jnp.broadcast_to(p, (LANES, LANES)).T
        ang = col * invf
        cos_ref[r * LANES:(r + 1) * LANES, :] = jnp.cos(ang)
        sin_ref[r * LANES:(r + 1) * LANES, :] = jnp.sin(ang)


def _rope_tables(positions, w_in_t):
    n_tok = positions.size
    rows = n_tok // LANES
    rb = 8
    pos2d = positions.reshape(rows, LANES)
    inv_freq = (ROPE_BASE ** (-jnp.arange(0, RET_DK, 2, dtype=F32) / RET_DK)).reshape(1, RET_DK // 2)
    out = jax.ShapeDtypeStruct((n_tok, RET_DK // 2), F32)
    n_steps = rows // rb
    win_in, win_out, win_shape, full_slabs, tail_rows = _win_cast_specs(w_in_t, 0, ROPE_WIN_SLAB, n_steps)
    return pl.pallas_call(
        functools.partial(_rope_kernel, win_full_slabs=full_slabs, win_tail_rows=tail_rows),
        grid=(n_steps,),
        in_specs=[pl.BlockSpec((rb, LANES), lambda i: (i, 0)),
                  pl.BlockSpec((1, RET_DK // 2), lambda i: (0, 0)),
                  win_in],
        out_specs=[pl.BlockSpec((rb * LANES, RET_DK // 2), lambda i: (i, 0))] * 2 + [win_out],
        out_shape=[out, out, win_shape],
        compiler_params=_params(1),
        name="rope_tables",
    )(pos2d, inv_freq, w_in_t)


GATE_LO = 2 * GLA_QK + 2 * GLA_V
GATE_HI = GATE_LO + GLA_GATE_RANK


def _inproj_kernel(x_ref, g_ref, w_ref, wgr_ref, gup_ref, gb_ref, cos_ref, sin_ref, wout_ref, wup_ref, wdown_ref,
                   proj_ref, loga_ref, wout_b_ref, wup_b_ref, wdown_b_ref, h_ref, *, q_tile, k_tile):
    j = pl.program_id(1)

    def project(h):
        return lax.dot_general(h, w_ref[...], _NT, preferred_element_type=F32)

    @pl.when(j == 0)
    def _():
        wout_b_ref[...] = wout_ref[...].astype(wout_b_ref.dtype)
        hb = _rms(x_ref[...], g_ref[...]).astype(BF16)
        h_ref[...] = hb
        gr = lax.dot_general(hb, wgr_ref[...], _NT, preferred_element_type=F32)
        z = jnp.dot(gr.astype(BF16), gup_ref[...], preferred_element_type=F32) + gb_ref[...]
        log_sig = jnp.minimum(z, 0.0) - jnp.log(1.0 + jnp.exp(-jnp.abs(z)))
        loga_ref[...] = log_sig * (1.0 / GLA_TAU)
        proj_ref[...] = project(hb).astype(proj_ref.dtype)

    is_rotary = (j == q_tile) | (j == k_tile)

    @pl.when((j != 0) & jnp.logical_not(is_rotary))
    def _():
        for k in range(wup_b_ref.shape[0]):
            cols = slice(k * MLP_FF_TILE, (k + 1) * MLP_FF_TILE)
            wup_b_ref[k] = wup_ref[:, cols].astype(wup_b_ref.dtype)
        wdown_b_ref[...] = wdown_ref[...].astype(wdown_b_ref.dtype)
        proj_ref[...] = project(h_ref[...]).astype(proj_ref.dtype)

    @pl.when(is_rotary)
    def _():
        p = project(h_ref[...])
        scale = jnp.where(j == k_tile, RET_DK ** -0.5, 1.0).astype(F32)
        cos = cos_ref[...] * scale
        sin = sin_ref[...] * scale
        half = RET_DK // 2
        for h in range(RET_HEADS):
            lo = slice(h * RET_DK, h * RET_DK + half)
            hi = slice(h * RET_DK + half, (h + 1) * RET_DK)
            t1, t2 = p[:, lo], p[:, hi]
            proj_ref[:, lo] = (t1 * cos - t2 * sin).astype(proj_ref.dtype)
            proj_ref[:, hi] = (t2 * cos + t1 * sin).astype(proj_ref.dtype)


def _inproj(x, gain, w_t, w_gr_t, gate_up, gate_bias, cos, sin, w_out, w_up, w_down, layer, tm=1024):
    n_tok = x.shape[0]
    n_i = n_tok // tm
    wout_rows = D_MODEL // n_i
    tn = RET_QK
    assert OFF_RQ % tn == 0 and OFF_RK % tn == 0 and GATE_LO % tn == 0 and OFF_RQ >= tn
    n_j = D_PROJ // tn
    plain_steps = [j for j in range(1, n_j) if j not in (OFF_RQ // tn, OFF_RK // tn)]
    n_slabs = n_i * len(plain_steps)
    up_rows, down_rows = D_MODEL // n_slabs, D_FF // n_slabs
    n_ff = D_FF // MLP_FF_TILE

    def slab(i, j):
        done = sum((j > p).astype(jnp.int32) for p in plain_steps)
        return i * len(plain_steps) + jnp.minimum(done, len(plain_steps) - 1)

    def w_rows(i, j):
        start = j * tn
        row = jnp.where(start < GATE_LO, start, start + GLA_GATE_RANK)
        return pl.multiple_of(row, GLA_GATE_RANK), 0

    return pl.pallas_call(
        functools.partial(_inproj_kernel, q_tile=OFF_RQ // tn, k_tile=OFF_RK // tn),
        grid=(n_i, n_j),
        in_specs=[pl.BlockSpec((tm, D_MODEL), lambda i, j: (i, 0)),
                  pl.BlockSpec((None, 1, D_MODEL), lambda i, j: (layer, 0, 0)),
                  pl.BlockSpec((pl.Element(tn), pl.Element(D_MODEL)), w_rows),
                  pl.BlockSpec((None, LANES, D_MODEL), lambda i, j: (layer, 0, 0)),
                  pl.BlockSpec((None, LANES, GLA_QK), lambda i, j: (layer, 0, 0)),
                  pl.BlockSpec((None, 1, GLA_QK), lambda i, j: (layer, 0, 0)),
                  pl.BlockSpec((tm, RET_DK // 2), lambda i, j: (i, 0)),
                  pl.BlockSpec((tm, RET_DK // 2), lambda i, j: (i, 0)),
                  pl.BlockSpec((None, wout_rows, D_MODEL), lambda i, j: (layer, i, 0)),
                  pl.BlockSpec((None, up_rows, D_FF), lambda i, j: (layer, slab(i, j), 0)),
                  pl.BlockSpec((None, down_rows, D_MODEL), lambda i, j: (layer, slab(i, j), 0))],
        out_specs=[pl.BlockSpec((tm, tn), lambda i, j: (i, j)),
                   pl.BlockSpec((tm, GLA_QK), lambda i, j: (i, 0)),
                   pl.BlockSpec((wout_rows, D_MODEL), lambda i, j: (i, 0)),
                   pl.BlockSpec((n_ff, up_rows, MLP_FF_TILE), lambda i, j: (0, slab(i, j), 0)),
                   pl.BlockSpec((down_rows, D_MODEL), lambda i, j: (slab(i, j), 0))],
        out_shape=[jax.ShapeDtypeStruct((n_tok, D_PROJ), BF16),
                   jax.ShapeDtypeStruct((n_tok, GLA_QK), F32),
                   jax.ShapeDtypeStruct((D_MODEL, D_MODEL), BF16),
                   jax.ShapeDtypeStruct((n_ff, D_MODEL, MLP_FF_TILE), BF16),
                   jax.ShapeDtypeStruct((D_FF, D_MODEL), BF16)],
        scratch_shapes=[pltpu.VMEM((tm, D_MODEL), BF16)],
        compiler_params=_params(2),
        name="norm_inproj",
    )(x, gain, w_t, w_gr_t, gate_up, gate_bias, cos, sin, w_out, w_up, w_down)


def _split3(x):
    hi = x.astype(BF16)
    r1 = x - hi.astype(F32)
    mid = r1.astype(BF16)
    lo = (r1 - mid.astype(F32)).astype(BF16)
    return hi, mid, lo


def _ret_gamma(h):
    return 1.0 - 2.0 ** (-5.0 - h)


def _ret_decay_tables(t_blk):
    assert t_blk == RET_DK == RET_DV
    log_gamma = np.log1p(-(2.0 ** (-5.0 - np.arange(RET_HEADS))))[:, None, None]
    idx = np.arange(t_blk, dtype=np.float64)
    rel = idx[:, None] - idx[None, :]
    decay = np.where(rel >= 0.0, np.exp(log_gamma * np.maximum(rel, 0.0)), 0.0)
    tail = np.broadcast_to(np.exp(log_gamma * (t_blk - 1.0 - idx[None, :, None])), decay.shape)
    inner = np.broadcast_to(np.exp(log_gamma * (idx[None, :, None] + 1.0)), decay.shape)
    return jnp.asarray(np.stack([decay, tail, inner], axis=1), dtype=F32)


def _silu(g):
    half_g = 0.5 * g
    return half_g + half_g * jnp.tanh(half_g)


def _mix_block(proj_ref, loga_ref, gno_ref, rgain_ref, rbias_ref, dec_ref, mix_ref, sg_ref, sr_ref):
    t_blk = MIX_BLOCK
    n_chunk = t_blk // GLA_CHUNK

    row = lax.broadcasted_iota(jnp.int32, (t_blk, t_blk), 0)
    col = lax.broadcasted_iota(jnp.int32, (t_blk, t_blk), 1)
    causal = col <= row
    chunk_shift = int(math.log2(GLA_CHUNK))
    gla_mask = causal & ((row >> chunk_shift) == (col >> chunk_shift))

    cum = gla_mask.astype(BF16)
    la_hi, la_mid, la_lo = _split3(loga_ref[...])
    b_all = (jnp.dot(cum, la_hi, preferred_element_type=F32)
             + jnp.dot(cum, la_mid, preferred_element_type=F32)
             + jnp.dot(cum, la_lo, preferred_element_type=F32))

    for h in range(GLA_HEADS):
        q = proj_ref[:, OFF_GQ + h * GLA_DK:OFF_GQ + (h + 1) * GLA_DK].astype(F32)
        k = proj_ref[:, OFF_GK + h * GLA_DK:OFF_GK + (h + 1) * GLA_DK].astype(F32)
        v = proj_ref[:, OFF_GV + h * GLA_DV:OFF_GV + (h + 1) * GLA_DV]
        g = proj_ref[:, OFF_GG + h * GLA_DV:OFF_GG + (h + 1) * GLA_DV].astype(F32)
        b = b_all[:, h * GLA_DK:(h + 1) * GLA_DK]
        b_last = [b[(c + 1) * GLA_CHUNK - 1:(c + 1) * GLA_CHUNK, :] for c in range(n_chunk)]
        pre = [jnp.zeros_like(b_last[0])]
        for c in range(1, n_chunk):
            pre.append(pre[-1] + b_last[c - 1])
        post = [jnp.zeros_like(b_last[0])]
        for c in range(n_chunk - 2, -1, -1):
            post.insert(0, post[0] + b_last[c + 1])
        by_rows = lambda vs: jnp.concatenate([jnp.broadcast_to(x, (GLA_CHUNK, GLA_DK)) for x in vs], axis=0)
        chunk = lambda a, c: a[c * GLA_CHUNK:(c + 1) * GLA_CHUNK]

        q_dec32 = q * (GLA_DK ** -0.5) * jnp.exp(b)
        q_dec = q_dec32.astype(BF16)
        q_blk = (q_dec32 * jnp.exp(by_rows(pre))).astype(BF16)
        k_dec = (k * jnp.exp(-b)).astype(BF16)
        k_tail32 = k * jnp.exp(by_rows(b_last) - b)
        k_tail = k_tail32.astype(BF16)
        k_end = (k_tail32 * jnp.exp(by_rows(post))).astype(BF16)

        s_rows = []
        for i in range(n_chunk):
            keys = []
            for j in range(n_chunk):
                if j + 1 == i:
                    keys.append(chunk(k_tail, j))
                elif j < i:
                    keys.append((chunk(k_tail32, j) * jnp.exp(pre[i] - pre[j + 1])).astype(BF16))
                else:
                    keys.append(chunk(k_dec, j))
            s_rows.append(lax.dot_general(chunk(q_dec, i), jnp.concatenate(keys, axis=0), _NT,
                                          preferred_element_type=F32))
        s = jnp.where(causal, jnp.concatenate(s_rows, axis=0), 0.0).astype(BF16)

        state = sg_ref[h]
        o = (jnp.dot(s, v, preferred_element_type=F32)
             + jnp.dot(q_blk, state.astype(BF16), preferred_element_type=F32))
        dcol = jnp.broadcast_to(jnp.exp(pre[-1] + b_last[-1]), (GLA_DK, GLA_DK)).T
        sg_ref[h] = (jnp.concatenate([dcol, dcol], axis=1) * state
                     + lax.dot_general(k_end, v, _TN, preferred_element_type=F32))

        y = o * lax.rsqrt(jnp.mean(o * o, axis=-1, keepdims=True) + EPS)
        y = y * gno_ref[:, h * GLA_DV:(h + 1) * GLA_DV]
        y = y * _silu(g)
        mix_ref[:, h * GLA_DV:(h + 1) * GLA_DV] = y.astype(mix_ref.dtype)

    for h in range(RET_HEADS):
        qb = proj_ref[:, OFF_RQ + h * RET_DK:OFF_RQ + (h + 1) * RET_DK]
        kb = proj_ref[:, OFF_RK + h * RET_DK:OFF_RK + (h + 1) * RET_DK]
        v = proj_ref[:, OFF_RV + h * RET_DV:OFF_RV + (h + 1) * RET_DV]
        g = proj_ref[:, OFF_RG + h * RET_DV:OFF_RG + (h + 1) * RET_DV].astype(F32)
        s = lax.dot_general(qb, kb, _NT, preferred_element_type=F32) * dec_ref[h, 0]
        o_intra = jnp.dot(s.astype(BF16), v, preferred_element_type=F32)
        k_tail = (kb.astype(F32) * dec_ref[h, 1]).astype(BF16)
        local = lax.dot_general(k_tail, v, _TN, preferred_element_type=F32)
        state = sr_ref[h]
        o_inter = jnp.dot(qb, state.astype(BF16), preferred_element_type=F32)
        o = o_intra + o_inter * dec_ref[h, 2]
        sr_ref[h] = _ret_gamma(h) ** t_blk * state + local

        mu = jnp.mean(o, axis=-1, keepdims=True)
        cen = o - mu
        y = cen * lax.rsqrt(jnp.mean(cen * cen, axis=-1, keepdims=True) + EPS)
        y = y * rgain_ref[:, h * RET_DV:(h + 1) * RET_DV] + rbias_ref[:, h * RET_DV:(h + 1) * RET_DV]
        y = y * _silu(g)
        mix_ref[:, GLA_V + h * RET_DV:GLA_V + (h + 1) * RET_DV] = y.astype(mix_ref.dtype)


def _mixer_outproj_kernel(proj_ref, loga_ref, gno_ref, rgain_ref, rbias_ref, dec_ref, x_ref, wout_ref,
                          *rest, blocks_per_seq, cast_next_win, win_full_slabs=0, win_tail_rows=0):
    if cast_next_win:
        win_ref, out_ref, win_b_ref, sg_ref, sr_ref, mix_a, mix_b = rest
    else:
        out_ref, sg_ref, sr_ref, mix_a, mix_b = rest
    t = pl.program_id(0)

    @pl.when(lax.rem(t, blocks_per_seq) == 0)
    def _():
        sg_ref[...] = jnp.zeros_like(sg_ref)
        sr_ref[...] = jnp.zeros_like(sr_ref)

    @pl.when(t == 0)
    def _():
        mix_b[...] = jnp.zeros_like(mix_b)

    if cast_next_win:
        _win_cast_step(t, win_ref, win_b_ref, win_full_slabs, win_tail_rows)

    def step(src, dst):
        for blk in range(MIX_STEP_BLOCKS):
            rows = pl.ds(blk * MIX_BLOCK, MIX_BLOCK)
            _mix_block(proj_ref.at[rows], loga_ref.at[rows], gno_ref, rgain_ref, rbias_ref, dec_ref,
                       dst.at[rows], sg_ref, sr_ref)
        out_ref[...] = x_ref[...] + jnp.dot(src[...], wout_ref[...], preferred_element_type=F32)

    n_blk = pl.num_programs(0) - 1
    parity = lax.rem(t, 2)

    @pl.when((parity == 0) & (t < n_blk))
    def _():
        step(mix_b, mix_a)

    @pl.when(parity == 1)
    def _():
        step(mix_a, mix_b)

    @pl.when(t == n_blk)
    def _():
        out_ref[...] = x_ref[...] + jnp.dot(mix_b[...], wout_ref[...], preferred_element_type=F32)


def _mixers_outproj(proj, loga, gla_norm, ret_gain, ret_bias, x, w_out, next_win_t, layer, seq):
    n_tok = proj.shape[0]
    t_blk = MIX_BLOCK * MIX_STEP_BLOCKS
    n_blk = n_tok // t_blk
    assert n_blk % 2 == 0 and seq % t_blk == 0
    cur = lambda t: (jnp.minimum(t, n_blk - 1), 0)
    prev = lambda t: (jnp.maximum(t - 1, 0), 0)
    const = lambda t: (layer, 0, 0)
    cast_next_win = next_win_t is not None
    static = dict(blocks_per_seq=seq // t_blk, cast_next_win=cast_next_win)

    in_specs = [pl.BlockSpec((t_blk, D_PROJ), cur),
                pl.BlockSpec((t_blk, GLA_QK), cur),
                pl.BlockSpec((None, 1, GLA_V), const),
                pl.BlockSpec((None, 1, RET_V), const),
                pl.BlockSpec((None, 1, RET_V), const),
                pl.BlockSpec((RET_HEADS, 3, MIX_BLOCK, MIX_BLOCK), lambda t: (0, 0, 0, 0)),
                pl.BlockSpec((t_blk, D_MODEL), prev),
                pl.BlockSpec((D_MODEL, D_MODEL), lambda t: (0, 0))]
    out_specs = [pl.BlockSpec((t_blk, D_MODEL), prev)]
    out_shape = [jax.ShapeDtypeStruct((n_tok, D_MODEL), F32)]
    operands = [proj, loga, gla_norm, ret_gain, ret_bias, _ret_decay_tables(MIX_BLOCK), x, w_out]
    if cast_next_win:
        w_in_t, next_layer = next_win_t
        win_in, win_out, win_shape, full_slabs, tail_rows = _win_cast_specs(w_in_t, next_layer, WIN_SLAB, n_blk)
        in_specs.append(win_in)
        out_specs.append(win_out)
        out_shape.append(win_shape)
        static.update(win_full_slabs=full_slabs, win_tail_rows=tail_rows)
        operands.append(w_in_t)

    return pl.pallas_call(
        functools.partial(_mixer_outproj_kernel, **static),
        grid=(n_blk + 1,),
        in_specs=in_specs,
        out_specs=out_specs,
        out_shape=out_shape,
        scratch_shapes=[pltpu.VMEM((GLA_HEADS, GLA_DK, GLA_DV), F32),
                        pltpu.VMEM((RET_HEADS, RET_DK, RET_DV), F32),
                        pltpu.VMEM((t_blk, D_MODEL), BF16),
                        pltpu.VMEM((t_blk, D_MODEL), BF16)],
        compiler_params=_params(1),
        name="mixers_outproj",
    )(*operands)


def _mlp_kernel(x_ref, g_ref, wu_ref, wd_ref, fg_ref, o_ref, h_ref, *, apply_final_norm):
    k = pl.program_id(1)

    def ff_tile(h):
        u = jnp.dot(h, wu_ref[...], preferred_element_type=F32)
        a = jnp.square(jnp.maximum(u, 0.0)).astype(BF16)
        return jnp.dot(a, wd_ref[...], preferred_element_type=F32)

    @pl.when(k == 0)
    def _():
        x = x_ref[...]
        h = _rms(x, g_ref[...]).astype(BF16)
        h_ref[...] = h
        o_ref[...] = x + ff_tile(h)

    @pl.when(k != 0)
    def _():
        o_ref[...] += ff_tile(h_ref[...])

    if apply_final_norm:
        @pl.when(k == pl.num_programs(1) - 1)
        def _():
            o_ref[...] = _rms(o_ref[...], fg_ref[...])


def _mlp(x, gain, w_up, w_down, final_gain, layer, apply_final_norm, tm=1024):
    n_tok = x.shape[0]
    tf = MLP_FF_TILE
    return pl.pallas_call(
        functools.partial(_mlp_kernel, apply_final_norm=apply_final_norm),
        grid=(n_tok // tm, D_FF // tf),
        in_specs=[pl.BlockSpec((tm, D_MODEL), lambda i, k: (i, 0)),
                  pl.BlockSpec((None, 1, D_MODEL), lambda i, k: (layer, 0, 0)),
                  pl.BlockSpec((None, D_MODEL, tf), lambda i, k: (k, 0, 0)),
                  pl.BlockSpec((tf, D_MODEL), lambda i, k: (k, 0)),
                  pl.BlockSpec((1, D_MODEL), lambda i, k: (0, 0))],
        out_specs=pl.BlockSpec((tm, D_MODEL), lambda i, k: (i, 0)),
        out_shape=jax.ShapeDtypeStruct((n_tok, D_MODEL), F32),
        scratch_shapes=[pltpu.VMEM((tm, D_MODEL), BF16)],
        compiler_params=_params(2),
        name="norm_mlp_residual",
    )(x, gain, w_up, w_down, final_gain)


def kernel(x, positions, attn_norm, w_in, gla_gate_up, gla_gate_bias, gla_out_norm,
           ret_norm_gain, ret_norm_bias, w_out, mlp_norm, w_up, w_down, final_norm):
    batch, seq, d_model = x.shape
    depth = w_in.shape[0]
    assert d_model == D_MODEL and seq % (MIX_BLOCK * MIX_STEP_BLOCKS) == 0
    n_tok = batch * seq
    xf = x.reshape(n_tok, d_model)

    pad = LANES - GLA_GATE_RANK
    w_in_t = jnp.swapaxes(w_in, 1, 2)
    cos, sin, w_t = _rope_tables(positions, w_in_t)
    w_gr_t = jnp.pad(w_in[:, :, GATE_LO:GATE_HI].swapaxes(1, 2), ((0, 0), (0, pad), (0, 0))).astype(BF16)
    gate_up = jnp.pad(gla_gate_up, ((0, 0), (0, pad), (0, 0))).astype(BF16)
    row3 = lambda a: a.reshape(depth, 1, -1)
    attn_g, gate_b, mlp_g = row3(attn_norm), row3(gla_gate_bias), row3(mlp_norm)
    gla_g, ret_g, ret_b = row3(gla_out_norm), row3(ret_norm_gain), row3(ret_norm_bias)
    final_g = final_norm.reshape(1, -1)

    for l in range(depth):
        last = l == depth - 1
        proj, loga, w_out_b, w_up_b, w_down_b = _inproj(xf, attn_g, w_t, w_gr_t, gate_up, gate_b, cos, sin,
                                                        w_out, w_up, w_down, l)
        outs = _mixers_outproj(proj, loga, gla_g, ret_g, ret_b, xf, w_out_b,
                               None if last else (w_in_t, l + 1), l, seq)
        xf = outs[0]
        if not last:
            w_t = outs[1]
        xf = _mlp(xf, mlp_g, w_up_b, w_down_b, final_g, l, apply_final_norm=last)
    return xf.reshape(batch, seq, d_model)
```

```python
import functools
import math

import jax
import jax.numpy as jnp
import numpy as np
from jax import lax
from jax.experimental import pallas as pl
from jax.experimental.pallas import tpu as pltpu

D_MODEL = 2048
GLA_HEADS = 4
GLA_DK = 128
GLA_DV = 256
GLA_GATE_RANK = 16
GLA_TAU = 16.0
GLA_CHUNK = 64
RET_HEADS = 4
RET_DK = 256
RET_DV = 256
ROPE_BASE = 10000.0
D_FF = 4 * D_MODEL
EPS = 1e-6

GLA_QK = GLA_HEADS * GLA_DK
GLA_V = GLA_HEADS * GLA_DV
RET_QK = RET_HEADS * RET_DK
RET_V = RET_HEADS * RET_DV
D_PROJ = 2 * GLA_QK + 2 * GLA_V + 2 * RET_QK + 2 * RET_V

OFF_GQ = 0
OFF_GK = OFF_GQ + GLA_QK
OFF_GV = OFF_GK + GLA_QK
OFF_GG = OFF_GV + GLA_V
OFF_RQ = OFF_GG + GLA_V
OFF_RK = OFF_RQ + RET_QK
OFF_RV = OFF_RK + RET_QK
OFF_RG = OFF_RV + RET_V

LANES = 128
MIX_BLOCK = 256
MIX_STEP_BLOCKS = 1
MLP_FF_TILE = 1024
WIN_SLAB = 128
ROPE_WIN_SLAB = 512
VMEM_LIMIT = 60 * 1024 * 1024

F32 = jnp.float32
BF16 = jnp.bfloat16

_NT = (((1,), (1,)), ((), ()))
_TN = (((0,), (0,)), ((), ()))


def _params(n_axes):
    return pltpu.CompilerParams(
        dimension_semantics=("arbitrary",) * n_axes, vmem_limit_bytes=VMEM_LIMIT)


def _rms(x, gain):
    return x * lax.rsqrt(jnp.mean(x * x, axis=-1, keepdims=True) + EPS) * gain


def _win_cast_specs(w_in_t, layer, slab, n_steps):
    d_in = w_in_t.shape[1]
    full_slabs, tail_rows = divmod(d_in, slab)
    n_slabs = full_slabs + (tail_rows > 0)
    last = d_in - slab
    assert last % GLA_GATE_RANK == 0 and tail_rows % GLA_GATE_RANK == 0 and n_slabs <= n_steps
    row = lambda t: pl.multiple_of(jnp.minimum(t * slab, last), GLA_GATE_RANK)
    in_spec = pl.BlockSpec((pl.Element(1), pl.Element(slab), pl.Element(D_MODEL)), lambda t: (layer, row(t), 0))
    out_spec = pl.BlockSpec((slab, D_MODEL), lambda t: (jnp.minimum(t, n_slabs - 1), 0))
    out_shape = jax.ShapeDtypeStruct((n_slabs * slab, D_MODEL), BF16)
    return in_spec, out_spec, out_shape, full_slabs, tail_rows


def _win_cast_step(t, win_ref, win_b_ref, full_slabs, tail_rows):
    slab = win_b_ref.shape[0]

    @pl.when(t < full_slabs)
    def _():
        win_b_ref[...] = win_ref[0].astype(win_b_ref.dtype)

    @pl.when(t >= full_slabs)
    def _():
        win_b_ref[...] = jnp.zeros_like(win_b_ref)
        win_b_ref[:tail_rows, :] = win_ref[0, slab - tail_rows:, :].astype(win_b_ref.dtype)


def _rope_kernel(pos_ref, invf_ref, win_ref, cos_ref, sin_ref, win_b_ref, *, win_full_slabs, win_tail_rows):
    _win_cast_step(pl.program_id(0), win_ref, win_b_ref, win_full_slabs, win_tail_rows)
    rows = pos_ref.shape[0]
    invf = invf_ref[...]
    for r in range(rows):
        p = pos_ref[r:r + 1, :].astype(F32)
        col = jnp.broadcast_to(p, (LANES, LANES)).T
        ang = col * invf
        cos_ref[r * LANES:(r + 1) * LANES, :] = jnp.cos(ang)
        sin_ref[r * LANES:(r + 1) * LANES, :] = jnp.sin(ang)


def _rope_tables(positions, w_in_t):
    n_tok = positions.size
    rows = n_tok // LANES
    rb = 8
    pos2d = positions.reshape(rows, LANES)
    inv_freq = (ROPE_BASE ** (-jnp.arange(0, RET_DK, 2, dtype=F32) / RET_DK)).reshape(1, RET_DK // 2)
    out = jax.ShapeDtypeStruct((n_tok, RET_DK // 2), F32)
    n_steps = rows // rb
    win_in, win_out, win_shape, full_slabs, tail_rows = _win_cast_specs(w_in_t, 0, ROPE_WIN_SLAB, n_steps)
    return pl.pallas_call(
        functools.partial(_rope_kernel, win_full_slabs=full_slabs, win_tail_rows=tail_rows),
        grid=(n_steps,),
        in_specs=[pl.BlockSpec((rb, LANES), lambda i: (i, 0)),
                  pl.BlockSpec((1, RET_DK // 2), lambda i: (0, 0)),
                  win_in],
        out_specs=[pl.BlockSpec((rb * LANES, RET_DK // 2), lambda i: (i, 0))] * 2 + [win_out],
        out_shape=[out, out, win_shape],
        compiler_params=_params(1),
        name="rope_tables",
    )(pos2d, inv_freq, w_in_t)


GATE_LO = 2 * GLA_QK + 2 * GLA_V
GATE_HI = GATE_LO + GLA_GATE_RANK


def _inproj_kernel(x_ref, g_ref, w_ref, wgr_ref, gup_ref, gb_ref, cos_ref, sin_ref, wout_ref,
                   proj_ref, loga_ref, wout_b_ref, h_a, h_b, *, q_tile, k_tile):
    j = pl.program_id(1)
    s = pl.program_id(2)

    def project(h):
        return lax.dot_general(h, w_ref[...], _NT, preferred_element_type=F32)

    def first(h_ref):
        wout_b_ref[...] = wout_ref[...].astype(wout_b_ref.dtype)
        hb = _rms(x_ref[...], g_ref[...]).astype(BF16)
        h_ref[...] = hb
        gr = lax.dot_general(hb, wgr_ref[...], _NT, preferred_element_type=F32)
        z = jnp.dot(gr.astype(BF16), gup_ref[...], preferred_element_type=F32) + gb_ref[...]
        log_sig = jnp.minimum(z, 0.0) - jnp.log(1.0 + jnp.exp(-jnp.abs(z)))
        loga_ref[...] = log_sig * (1.0 / GLA_TAU)
        proj_ref[...] = project(hb).astype(proj_ref.dtype)

    def plain(h_ref):
        proj_ref[...] = project(h_ref[...]).astype(proj_ref.dtype)

    def rotary(h_ref):
        p = project(h_ref[...])
        scale = jnp.where(j == k_tile, RET_DK ** -0.5, 1.0).astype(F32)
        cos = cos_ref[...] * scale
        sin = sin_ref[...] * scale
        half = RET_DK // 2
        for h in range(RET_HEADS):
            lo = slice(h * RET_DK, h * RET_DK + half)
            hi = slice(h * RET_DK + half, (h + 1) * RET_DK)
            t1, t2 = p[:, lo], p[:, hi]
            proj_ref[:, lo] = (t1 * cos - t2 * sin).astype(proj_ref.dtype)
            proj_ref[:, hi] = (t2 * cos + t1 * sin).astype(proj_ref.dtype)

    is_rotary = (j == q_tile) | (j == k_tile)
    for member, h_ref in ((0, h_a), (1, h_b)):
        mine = s == member
        pl.when(mine & (j == 0))(functools.partial(first, h_ref))
        pl.when(mine & (j != 0) & jnp.logical_not(is_rotary))(functools.partial(plain, h_ref))
        pl.when(mine & is_rotary)(functools.partial(rotary, h_ref))


def _inproj(x, gain, w_t, w_gr_t, gate_up, gate_bias, cos, sin, w_out, layer, tm=1024):
    n_tok = x.shape[0]
    n_i = n_tok // tm
    assert n_i % 2 == 0
    wout_rows = D_MODEL // n_i
    tn = RET_QK
    assert OFF_RQ % tn == 0 and OFF_RK % tn == 0 and GATE_LO % tn == 0 and OFF_RQ >= tn
    q_tile, k_tile = OFF_RQ // tn, OFF_RK // tn

    def w_rows(p, j, s):
        start = j * tn
        row = jnp.where(start < GATE_LO, start, start + GLA_GATE_RANK)
        return pl.multiple_of(row, GLA_GATE_RANK), 0

    tile = lambda p, j, s: 2 * p + s
    at_first = lambda p, j, s: 2 * p + jnp.where(j == 0, s, 1)
    at_rotary = lambda p, j, s: 2 * p + jnp.where((j == q_tile) | (j == k_tile), s, 1)

    return pl.pallas_call(
        functools.partial(_inproj_kernel, q_tile=q_tile, k_tile=k_tile),
        grid=(n_i // 2, D_PROJ // tn, 2),
        in_specs=[pl.BlockSpec((tm, D_MODEL), lambda p, j, s: (at_first(p, j, s), 0)),
                  pl.BlockSpec((None, 1, D_MODEL), lambda p, j, s: (layer, 0, 0)),
                  pl.BlockSpec((pl.Element(tn), pl.Element(D_MODEL)), w_rows),
                  pl.BlockSpec((None, LANES, D_MODEL), lambda p, j, s: (layer, 0, 0)),
                  pl.BlockSpec((None, LANES, GLA_QK), lambda p, j, s: (layer, 0, 0)),
                  pl.BlockSpec((None, 1, GLA_QK), lambda p, j, s: (layer, 0, 0)),
                  pl.BlockSpec((tm, RET_DK // 2), lambda p, j, s: (at_rotary(p, j, s), 0)),
                  pl.BlockSpec((tm, RET_DK // 2), lambda p, j, s: (at_rotary(p, j, s), 0)),
                  pl.BlockSpec((None, wout_rows, D_MODEL), lambda p, j, s: (layer, at_first(p, j, s), 0))],
        out_specs=[pl.BlockSpec((tm, tn), lambda p, j, s: (tile(p, j, s), j)),
                   pl.BlockSpec((tm, GLA_QK), lambda p, j, s: (at_first(p, j, s), 0)),
                   pl.BlockSpec((wout_rows, D_MODEL), lambda p, j, s: (at_first(p, j, s), 0))],
        out_shape=[jax.ShapeDtypeStruct((n_tok, D_PROJ), BF16),
                   jax.ShapeDtypeStruct((n_tok, GLA_QK), F32),
                   jax.ShapeDtypeStruct((D_MODEL, D_MODEL), BF16)],
        scratch_shapes=[pltpu.VMEM((tm, D_MODEL), BF16), pltpu.VMEM((tm, D_MODEL), BF16)],
        compiler_params=_params(3),
        name="norm_inproj",
    )(x, gain, w_t, w_gr_t, gate_up, gate_bias, cos, sin, w_out)


def _split3(x):
    hi = x.astype(BF16)
    r1 = x - hi.astype(F32)
    mid = r1.astype(BF16)
    lo = (r1 - mid.astype(F32)).astype(BF16)
    return hi, mid, lo


def _ret_gamma(h):
    return 1.0 - 2.0 ** (-5.0 - h)


def _ret_decay_tables(t_blk):
    assert t_blk == RET_DK == RET_DV
    log_gamma = np.log1p(-(2.0 ** (-5.0 - np.arange(RET_HEADS))))[:, None, None]
    idx = np.arange(t_blk, dtype=np.float64)
    rel = idx[:, None] - idx[None, :]
    decay = np.where(rel >= 0.0, np.exp(log_gamma * np.maximum(rel, 0.0)), 0.0)
    tail = np.broadcast_to(np.exp(log_gamma * (t_blk - 1.0 - idx[None, :, None])), decay.shape)
    inner = np.broadcast_to(np.exp(log_gamma * (idx[None, :, None] + 1.0)), decay.shape)
    return jnp.asarray(np.stack([decay, tail, inner], axis=1), dtype=F32)


def _silu(g):
    half_g = 0.5 * g
    return half_g + half_g * jnp.tanh(half_g)


def _mix_block(proj_ref, loga_ref, gno_ref, rgain_ref, rbias_ref, dec_ref, mix_ref, sg_ref, sr_ref):
    t_blk = MIX_BLOCK
    n_chunk = t_blk // GLA_CHUNK

    row = lax.broadcasted_iota(jnp.int32, (t_blk, t_blk), 0)
    col = lax.broadcasted_iota(jnp.int32, (t_blk, t_blk), 1)
    causal = col <= row
    chunk_shift = int(math.log2(GLA_CHUNK))
    gla_mask = causal & ((row >> chunk_shift) == (col >> chunk_shift))

    cum = gla_mask.astype(BF16)
    la_hi, la_mid, la_lo = _split3(loga_ref[...])
    b_all = (jnp.dot(cum, la_hi, preferred_element_type=F32)
             + jnp.dot(cum, la_mid, preferred_element_type=F32)
             + jnp.dot(cum, la_lo, preferred_element_type=F32))

    for h in range(GLA_HEADS):
        q = proj_ref[:, OFF_GQ + h * GLA_DK:OFF_GQ + (h + 1) * GLA_DK].astype(F32)
        k = proj_ref[:, OFF_GK + h * GLA_DK:OFF_GK + (h + 1) * GLA_DK].astype(F32)
        v = proj_ref[:, OFF_GV + h * GLA_DV:OFF_GV + (h + 1) * GLA_DV]
        g = proj_ref[:, OFF_GG + h * GLA_DV:OFF_GG + (h + 1) * GLA_DV].astype(F32)
        b = b_all[:, h * GLA_DK:(h + 1) * GLA_DK]
        b_last = [b[(c + 1) * GLA_CHUNK - 1:(c + 1) * GLA_CHUNK, :] for c in range(n_chunk)]
        pre = [jnp.zeros_like(b_last[0])]
        for c in range(1, n_chunk):
            pre.append(pre[-1] + b_last[c - 1])
        post = [jnp.zeros_like(b_last[0])]
        for c in range(n_chunk - 2, -1, -1):
            post.insert(0, post[0] + b_last[c + 1])
        by_rows = lambda vs: jnp.concatenate([jnp.broadcast_to(x, (GLA_CHUNK, GLA_DK)) for x in vs], axis=0)
        chunk = lambda a, c: a[c * GLA_CHUNK:(c + 1) * GLA_CHUNK]

        q_dec32 = q * (GLA_DK ** -0.5) * jnp.exp(b)
        q_dec = q_dec32.astype(BF16)
        q_blk = (q_dec32 * jnp.exp(by_rows(pre))).astype(BF16)
        k_dec = (k * jnp.exp(-b)).astype(BF16)
        k_tail32 = k * jnp.exp(by_rows(b_last) - b)
        k_tail = k_tail32.astype(BF16)
        k_end = (k_tail32 * jnp.exp(by_rows(post))).astype(BF16)

        s_rows = []
        for i in range(n_chunk):
            keys = []
            for j in range(n_chunk):
                if j + 1 == i:
                    keys.append(chunk(k_tail, j))
                elif j < i:
                    keys.append((chunk(k_tail32, j) * jnp.exp(pre[i] - pre[j + 1])).astype(BF16))
                else:
                    keys.append(chunk(k_dec, j))
            s_rows.append(lax.dot_general(chunk(q_dec, i), jnp.concatenate(keys, axis=0), _NT,
                                          preferred_element_type=F32))
        s = jnp.where(causal, jnp.concatenate(s_rows, axis=0), 0.0).astype(BF16)

        state = sg_ref[h]
        o = (jnp.dot(s, v, preferred_element_type=F32)
             + jnp.dot(q_blk, state.astype(BF16), preferred_element_type=F32))
        dcol = jnp.broadcast_to(jnp.exp(pre[-1] + b_last[-1]), (GLA_DK, GLA_DK)).T
        sg_ref[h] = (jnp.concatenate([dcol, dcol], axis=1) * state
                     + lax.dot_general(k_end, v, _TN, preferred_element_type=F32))

        y = o * lax.rsqrt(jnp.mean(o * o, axis=-1, keepdims=True) + EPS)
        y = y * gno_ref[:, h * GLA_DV:(h + 1) * GLA_DV]
        y = y * _silu(g)
        mix_ref[:, h * GLA_DV:(h + 1) * GLA_DV] = y.astype(mix_ref.dtype)

    for h in range(RET_HEADS):
        qb = proj_ref[:, OFF_RQ + h * RET_DK:OFF_RQ + (h + 1) * RET_DK]
        kb = proj_ref[:, OFF_RK + h * RET_DK:OFF_RK + (h + 1) * RET_DK]
        v = proj_ref[:, OFF_RV + h * RET_DV:OFF_RV + (h + 1) * RET_DV]
        g = proj_ref[:, OFF_RG + h * RET_DV:OFF_RG + (h + 1) * RET_DV].astype(F32)
        s = lax.dot_general(qb, kb, _NT, preferred_element_type=F32) * dec_ref[h, 0]
        o_intra = jnp.dot(s.astype(BF16), v, preferred_element_type=F32)
        k_tail = (kb.astype(F32) * dec_ref[h, 1]).astype(BF16)
        local = lax.dot_general(k_tail, v, _TN, preferred_element_type=F32)
        state = sr_ref[h]
        o_inter = jnp.dot(qb, state.astype(BF16), preferred_element_type=F32)
        o = o_intra + o_inter * dec_ref[h, 2]
        sr_ref[h] = _ret_gamma(h) ** t_blk * state + local

        mu = jnp.mean(o, axis=-1, keepdims=True)
        cen = o - mu
        y = cen * lax.rsqrt(jnp.mean(cen * cen, axis=-1, keepdims=True) + EPS)
        y = y * rgain_ref[:, h * RET_DV:(h + 1) * RET_DV] + rbias_ref[:, h * RET_DV:(h + 1) * RET_DV]
        y = y * _silu(g)
        mix_ref[:, GLA_V + h * RET_DV:GLA_V + (h + 1) * RET_DV] = y.astype(mix_ref.dtype)


def _mixer_outproj_kernel(proj_ref, loga_ref, gno_ref, rgain_ref, rbias_ref, dec_ref, x_ref, wout_ref,
                          wup_ref, wdown_ref, *rest, blocks_per_seq, cast_next_win,
                          win_full_slabs=0, win_tail_rows=0):
    if cast_next_win:
        win_ref, out_ref, wup_b_ref, wdown_b_ref, win_b_ref, sg_ref, sr_ref, mix_a, mix_b = rest
    else:
        out_ref, wup_b_ref, wdown_b_ref, sg_ref, sr_ref, mix_a, mix_b = rest
    t = pl.program_id(0)

    @pl.when(lax.rem(t, blocks_per_seq) == 0)
    def _():
        sg_ref[...] = jnp.zeros_like(sg_ref)
        sr_ref[...] = jnp.zeros_like(sr_ref)

    @pl.when(t == 0)
    def _():
        mix_b[...] = jnp.zeros_like(mix_b)

    if cast_next_win:
        _win_cast_step(t, win_ref, win_b_ref, win_full_slabs, win_tail_rows)

    def step(src, dst):
        for k in range(wup_b_ref.shape[0]):
            cols = slice(k * MLP_FF_TILE, (k + 1) * MLP_FF_TILE)
            wup_b_ref[k] = wup_ref[:, cols].astype(wup_b_ref.dtype)
        wdown_b_ref[...] = wdown_ref[...].astype(wdown_b_ref.dtype)
        for blk in range(MIX_STEP_BLOCKS):
            rows = pl.ds(blk * MIX_BLOCK, MIX_BLOCK)
            _mix_block(proj_ref.at[rows], loga_ref.at[rows], gno_ref, rgain_ref, rbias_ref, dec_ref,
                       dst.at[rows], sg_ref, sr_ref)
        out_ref[...] = x_ref[...] + jnp.dot(src[...], wout_ref[...], preferred_element_type=F32)

    n_blk = pl.num_programs(0) - 1
    parity = lax.rem(t, 2)

    @pl.when((parity == 0) & (t < n_blk))
    def _():
        step(mix_b, mix_a)

    @pl.when(parity == 1)
    def _():
        step(mix_a, mix_b)

    @pl.when(t == n_blk)
    def _():
        out_ref[...] = x_ref[...] + jnp.dot(mix_b[...], wout_ref[...], preferred_element_type=F32)


def _mixers_outproj(proj, loga, gla_norm, ret_gain, ret_bias, x, w_out, w_up, w_down, next_win_t, layer, seq):
    n_tok = proj.shape[0]
    t_blk = MIX_BLOCK * MIX_STEP_BLOCKS
    n_blk = n_tok // t_blk
    assert n_blk % 2 == 0 and seq % t_blk == 0
    n_ff = D_FF // MLP_FF_TILE
    cur = lambda t: (jnp.minimum(t, n_blk - 1), 0)
    prev = lambda t: (jnp.maximum(t - 1, 0), 0)
    const = lambda t: (layer, 0, 0)
    slab = lambda t: (layer, jnp.minimum(t, n_blk - 1), 0)
    up_rows, down_rows = D_MODEL // n_blk, D_FF // n_blk
    cast_next_win = next_win_t is not None
    static = dict(blocks_per_seq=seq // t_blk, cast_next_win=cast_next_win)

    in_specs = [pl.BlockSpec((t_blk, D_PROJ), cur),
                pl.BlockSpec((t_blk, GLA_QK), cur),
                pl.BlockSpec((None, 1, GLA_V), const),
                pl.BlockSpec((None, 1, RET_V), const),
                pl.BlockSpec((None, 1, RET_V), const),
                pl.BlockSpec((RET_HEADS, 3, MIX_BLOCK, MIX_BLOCK), lambda t: (0, 0, 0, 0)),
                pl.BlockSpec((t_blk, D_MODEL), prev),
                pl.BlockSpec((D_MODEL, D_MODEL), lambda t: (0, 0)),
                pl.BlockSpec((None, up_rows, D_FF), slab),
                pl.BlockSpec((None, down_rows, D_MODEL), slab)]
    out_specs = [pl.BlockSpec((t_blk, D_MODEL), prev),
                 pl.BlockSpec((n_ff, up_rows, MLP_FF_TILE), lambda t: (0, jnp.minimum(t, n_blk - 1), 0)),
                 pl.BlockSpec((down_rows, D_MODEL), cur)]
    out_shape = [jax.ShapeDtypeStruct((n_tok, D_MODEL), F32),
                 jax.ShapeDtypeStruct((n_ff, D_MODEL, MLP_FF_TILE), BF16),
                 jax.ShapeDtypeStruct((D_FF, D_MODEL), BF16)]
    operands = [proj, loga, gla_norm, ret_gain, ret_bias, _ret_decay_tables(MIX_BLOCK), x, w_out, w_up, w_down]
    if cast_next_win:
        w_in_t, next_layer = next_win_t
        win_in, win_out, win_shape, full_slabs, tail_rows = _win_cast_specs(w_in_t, next_layer, WIN_SLAB, n_blk)
        in_specs.append(win_in)
        out_specs.append(win_out)
        out_shape.append(win_shape)
        static.update(win_full_slabs=full_slabs, win_tail_rows=tail_rows)
        operands.append(w_in_t)

    return pl.pallas_call(
        functools.partial(_mixer_outproj_kernel, **static),
        grid=(n_blk + 1,),
        in_specs=in_specs,
        out_specs=out_specs,
        out_shape=out_shape,
        scratch_shapes=[pltpu.VMEM((GLA_HEADS, GLA_DK, GLA_DV), F32),
                        pltpu.VMEM((RET_HEADS, RET_DK, RET_DV), F32),
                        pltpu.VMEM((t_blk, D_MODEL), BF16),
                        pltpu.VMEM((t_blk, D_MODEL), BF16)],
        compiler_params=_params(1),
        name="mixers_outproj",
    )(*operands)


def _mlp_kernel(x_ref, g_ref, wu_ref, wd_ref, fg_ref, o_ref, h_ref, *, apply_final_norm):
    k = pl.program_id(1)

    def ff_tile(h):
        u = jnp.dot(h, wu_ref[...], preferred_element_type=F32)
        a = jnp.square(jnp.maximum(u, 0.0)).astype(BF16)
        return jnp.dot(a, wd_ref[...], preferred_element_type=F32)

    @pl.when(k == 0)
    def _():
        x = x_ref[...]
        h = _rms(x, g_ref[...]).astype(BF16)
        h_ref[...] = h
        o_ref[...] = x + ff_tile(h)

    @pl.when(k != 0)
    def _():
        o_ref[...] += ff_tile(h_ref[...])

    if apply_final_norm:
        @pl.when(k == pl.num_programs(1) - 1)
        def _():
            o_ref[...] = _rms(o_ref[...], fg_ref[...])


def _mlp(x, gain, w_up, w_down, final_gain, layer, apply_final_norm, tm=1024):
    n_tok = x.shape[0]
    tf = MLP_FF_TILE
    return pl.pallas_call(
        functools.partial(_mlp_kernel, apply_final_norm=apply_final_norm),
        grid=(n_tok // tm, D_FF // tf),
        in_specs=[pl.BlockSpec((tm, D_MODEL), lambda i, k: (i, 0)),
                  pl.BlockSpec((None, 1, D_MODEL), lambda i, k: (layer, 0, 0)),
                  pl.BlockSpec((None, D_MODEL, tf), lambda i, k: (k, 0, 0)),
                  pl.BlockSpec((tf, D_MODEL), lambda i, k: (k, 0)),
                  pl.BlockSpec((1, D_MODEL), lambda i, k: (0, 0))],
        out_specs=pl.BlockSpec((tm, D_MODEL), lambda i, k: (i, 0)),
        out_shape=jax.ShapeDtypeStruct((n_tok, D_MODEL), F32),
        scratch_shapes=[pltpu.VMEM((tm, D_MODEL), BF16)],
        compiler_params=_params(2),
        name="norm_mlp_residual",
    )(x, gain, w_up, w_down, final_gain)


def kernel(x, positions, attn_norm, w_in, gla_gate_up, gla_gate_bias, gla_out_norm,
           ret_norm_gain, ret_norm_bias, w_out, mlp_norm, w_up, w_down, final_norm):
    batch, seq, d_model = x.shape
    depth = w_in.shape[0]
    assert d_model == D_MODEL and seq % (MIX_BLOCK * MIX_STEP_BLOCKS) == 0
    n_tok = batch * seq
    xf = x.reshape(n_tok, d_model)

    pad = LANES - GLA_GATE_RANK
    w_in_t = jnp.swapaxes(w_in, 1, 2)
    cos, sin, w_t = _rope_tables(positions, w_in_t)
    w_gr_t = jnp.pad(w_in[:, :, GATE_LO:GATE_HI].swapaxes(1, 2), ((0, 0), (0, pad), (0, 0))).astype(BF16)
    gate_up = jnp.pad(gla_gate_up, ((0, 0), (0, pad), (0, 0))).astype(BF16)
    row3 = lambda a: a.reshape(depth, 1, -1)
    attn_g, gate_b, mlp_g = row3(attn_norm), row3(gla_gate_bias), row3(mlp_norm)
    gla_g, ret_g, ret_b = row3(gla_out_norm), row3(ret_norm_gain), row3(ret_norm_bias)
    final_g = final_norm.reshape(1, -1)

    for l in range(depth):
        last = l == depth - 1
        proj, loga, w_out_b = _inproj(xf, attn_g, w_t, w_gr_t, gate_up, gate_b, cos, sin, w_out, l)
        outs = _mixers_outproj(proj, loga, gla_g, ret_g, ret_b, xf, w_out_b, w_up, w_down,
                               None if last else (w_in_t, l + 1), l, seq)
        xf, w_up_b, w_down_b = outs[:3]
        if not last:
            w_t = outs[3]
        xf = _mlp(xf, mlp_g, w_up_b, w_down_b, final_g, l, apply_final_norm=last)
    return xf.reshape(batch, seq, d_model)
```

```python
import functools
import math

import jax
import jax.numpy as jnp
import numpy as np
from jax import lax
from jax.experimental import pallas as pl
from jax.experimental.pallas import tpu as pltpu

D_MODEL = 2048
GLA_HEADS = 4
GLA_DK = 128
GLA_DV = 256
GLA_GATE_RANK = 16
GLA_TAU = 16.0
GLA_CHUNK = 64
RET_HEADS = 4
RET_DK = 256
RET_DV = 256
ROPE_BASE = 10000.0
D_FF = 4 * D_MODEL
EPS = 1e-6

GLA_QK = GLA_HEADS * GLA_DK
GLA_V = GLA_HEADS * GLA_DV
RET_QK = RET_HEADS * RET_DK
RET_V = RET_HEADS * RET_DV
D_PROJ = 2 * GLA_QK + 2 * GLA_V + 2 * RET_QK + 2 * RET_V

OFF_GQ = 0
OFF_GK = OFF_GQ + GLA_QK
OFF_GV = OFF_GK + GLA_QK
OFF_GG = OFF_GV + GLA_V
OFF_RQ = OFF_GG + GLA_V
OFF_RK = OFF_RQ + RET_QK
OFF_RV = OFF_RK + RET_QK
OFF_RG = OFF_RV + RET_V

LANES = 128
MIX_BLOCK = 256
MIX_STEP_BLOCKS = 1
MLP_FF_TILE = 1024
WIN_SLAB = 128
ROPE_WIN_SLAB = 512
VMEM_LIMIT = 60 * 1024 * 1024

F32 = jnp.float32
BF16 = jnp.bfloat16

_NT = (((1,), (1,)), ((), ()))
_TN = (((0,), (0,)), ((), ()))


def _params(n_axes):
    return pltpu.CompilerParams(
        dimension_semantics=("arbitrary",) * n_axes, vmem_limit_bytes=VMEM_LIMIT)


def _rms(x, gain):
    return x * lax.rsqrt(jnp.mean(x * x, axis=-1, keepdims=True) + EPS) * gain


def _win_cast_specs(w_in_t, layer, slab, n_steps):
    d_in = w_in_t.shape[1]
    full_slabs, tail_rows = divmod(d_in, slab)
    n_slabs = full_slabs + (tail_rows > 0)
    last = d_in - slab
    assert last % GLA_GATE_RANK == 0 and tail_rows % GLA_GATE_RANK == 0 and n_slabs <= n_steps
    row = lambda t: pl.multiple_of(jnp.minimum(t * slab, last), GLA_GATE_RANK)
    in_spec = pl.BlockSpec((pl.Element(1), pl.Element(slab), pl.Element(D_MODEL)), lambda t: (layer, row(t), 0))
    out_spec = pl.BlockSpec((slab, D_MODEL), lambda t: (jnp.minimum(t, n_slabs - 1), 0))
    out_shape = jax.ShapeDtypeStruct((n_slabs * slab, D_MODEL), BF16)
    return in_spec, out_spec, out_shape, full_slabs, tail_rows


def _win_cast_step(t, win_ref, win_b_ref, full_slabs, tail_rows):
    slab = win_b_ref.shape[0]

    @pl.when(t < full_slabs)
    def _():
        win_b_ref[...] = win_ref[0].astype(win_b_ref.dtype)

    @pl.when(t >= full_slabs)
    def _():
        win_b_ref[...] = jnp.zeros_like(win_b_ref)
        win_b_ref[:tail_rows, :] = win_ref[0, slab - tail_rows:, :].astype(win_b_ref.dtype)


def _rope_kernel(pos_ref, invf_ref, win_ref, cos_ref, sin_ref, win_b_ref, *, win_full_slabs, win_tail_rows):
    _win_cast_step(pl.program_id(0), win_ref, win_b_ref, win_full_slabs, win_tail_rows)
    rows = pos_ref.shape[0]
    invf = invf_ref[...]
    for r in range(rows):
        p = pos_ref[r:r + 1, :].astype(F32)
        col = jnp.broadcast_to(p, (LANES, LANES)).T
        ang = col * invf
        cos_ref[r * LANES:(r + 1) * LANES, :] = jnp.cos(ang)
        sin_ref[r * LANES:(r + 1) * LANES, :] = jnp.sin(ang)


def _rope_tables(positions, w_in_t):
    n_tok = positions.size
    rows = n_tok // LANES
    rb = 8
    pos2d = positions.reshape(rows, LANES)
    inv_freq = (ROPE_BASE ** (-jnp.arange(0, RET_DK, 2, dtype=F32) / RET_DK)).reshape(1, RET_DK // 2)
    out = jax.ShapeDtypeStruct((n_tok, RET_DK // 2), F32)
    n_steps = rows // rb
    win_in, win_out, win_shape, full_slabs, tail_rows = _win_cast_specs(w_in_t, 0, ROPE_WIN_SLAB, n_steps)
    return pl.pallas_call(
        functools.partial(_rope_kernel, win_full_slabs=full_slabs, win_tail_rows=tail_rows),
        grid=(n_steps,),
        in_specs=[pl.BlockSpec((rb, LANES), lambda i: (i, 0)),
                  pl.BlockSpec((1, RET_DK // 2), lambda i: (0, 0)),
                  win_in],
        out_specs=[pl.BlockSpec((rb * LANES, RET_DK // 2), lambda i: (i, 0))] * 2 + [win_out],
        out_shape=[out, out, win_shape],
        compiler_params=_params(1),
        name="rope_tables",
    )(pos2d, inv_freq, w_in_t)


GATE_LO = 2 * GLA_QK + 2 * GLA_V
GATE_HI = GATE_LO + GLA_GATE_RANK


def _inproj_kernel(x_ref, g_ref, w_ref, wgr_ref, gup_ref, gb_ref, cos_ref, sin_ref, wout_ref,
                   proj_ref, loga_ref, wout_b_ref, h_ref, *, q_tile, k_tile):
    j = pl.program_id(1)

    def project(h):
        return lax.dot_general(h, w_ref[...], _NT, preferred_element_type=F32)

    @pl.when(j == 0)
    def _():
        wout_b_ref[...] = wout_ref[...].astype(wout_b_ref.dtype)
        hb = _rms(x_ref[...], g_ref[...]).astype(BF16)
        h_ref[...] = hb
        gr = lax.dot_general(hb, wgr_ref[...], _NT, preferred_element_type=F32)
        z = jnp.dot(gr.astype(BF16), gup_ref[...], preferred_element_type=F32) + gb_ref[...]
        log_sig = jnp.minimum(z, 0.0) - jnp.log(1.0 + jnp.exp(-jnp.abs(z)))
        loga_ref[...] = log_sig * (1.0 / GLA_TAU)
        proj_ref[...] = project(hb).astype(proj_ref.dtype)

    is_rotary = (j == q_tile) | (j == k_tile)

    @pl.when((j != 0) & jnp.logical_not(is_rotary))
    def _():
        proj_ref[...] = project(h_ref[...]).astype(proj_ref.dtype)

    @pl.when(is_rotary)
    def _():
        p = project(h_ref[...])
        scale = jnp.where(j == k_tile, RET_DK ** -0.5, 1.0).astype(F32)
        cos = cos_ref[...] * scale
        sin = sin_ref[...] * scale
        half = RET_DK // 2
        for h in range(RET_HEADS):
            lo = slice(h * RET_DK, h * RET_DK + half)
            hi = slice(h * RET_DK + half, (h + 1) * RET_DK)
            t1, t2 = p[:, lo], p[:, hi]
            proj_ref[:, lo] = (t1 * cos - t2 * sin).astype(proj_ref.dtype)
            proj_ref[:, hi] = (t2 * cos + t1 * sin).astype(proj_ref.dtype)


def _inproj(x, gain, w_t, w_gr_t, gate_up, gate_bias, cos, sin, w_out, layer, tm=1024):
    n_tok = x.shape[0]
    wout_rows = D_MODEL // (n_tok // tm)
    tn = RET_QK
    assert OFF_RQ % tn == 0 and OFF_RK % tn == 0 and GATE_LO % tn == 0 and OFF_RQ >= tn

    def w_rows(i, j):
        start = j * tn
        row = jnp.where(start < GATE_LO, start, start + GLA_GATE_RANK)
        return pl.multiple_of(row, GLA_GATE_RANK), 0

    return pl.pallas_call(
        functools.partial(_inproj_kernel, q_tile=OFF_RQ // tn, k_tile=OFF_RK // tn),
        grid=(n_tok // tm, D_PROJ // tn),
        in_specs=[pl.BlockSpec((tm, D_MODEL), lambda i, j: (i, 0)),
                  pl.BlockSpec((None, 1, D_MODEL), lambda i, j: (layer, 0, 0)),
                  pl.BlockSpec((pl.Element(tn), pl.Element(D_MODEL)), w_rows),
                  pl.BlockSpec((None, LANES, D_MODEL), lambda i, j: (layer, 0, 0)),
                  pl.BlockSpec((None, LANES, GLA_QK), lambda i, j: (layer, 0, 0)),
                  pl.BlockSpec((None, 1, GLA_QK), lambda i, j: (layer, 0, 0)),
                  pl.BlockSpec((tm, RET_DK // 2), lambda i, j: (i, 0)),
                  pl.BlockSpec((tm, RET_DK // 2), lambda i, j: (i, 0)),
                  pl.BlockSpec((None, wout_rows, D_MODEL), lambda i, j: (layer, i, 0))],
        out_specs=[pl.BlockSpec((tm, tn), lambda i, j: (i, j)),
                   pl.BlockSpec((tm, GLA_QK), lambda i, j: (i, 0)),
                   pl.BlockSpec((wout_rows, D_MODEL), lambda i, j: (i, 0))],
        out_shape=[jax.ShapeDtypeStruct((n_tok, D_PROJ), BF16),
                   jax.ShapeDtypeStruct((n_tok, GLA_QK), F32),
                   jax.ShapeDtypeStruct((D_MODEL, D_MODEL), BF16)],
        scratch_shapes=[pltpu.VMEM((tm, D_MODEL), BF16)],
        compiler_params=_params(2),
        name="norm_inproj",
    )(x, gain, w_t, w_gr_t, gate_up, gate_bias, cos, sin, w_out)


def _split3(x):
    hi = x.astype(BF16)
    r1 = x - hi.astype(F32)
    mid = r1.astype(BF16)
    lo = (r1 - mid.astype(F32)).astype(BF16)
    return hi, mid, lo


def _ret_gamma(h):
    return 1.0 - 2.0 ** (-5.0 - h)


def _ret_decay_tables(t_blk):
    assert t_blk == RET_DK == RET_DV
    log_gamma = np.log1p(-(2.0 ** (-5.0 - np.arange(RET_HEADS))))[:, None, None]
    idx = np.arange(t_blk, dtype=np.float64)
    rel = idx[:, None] - idx[None, :]
    decay = np.where(rel >= 0.0, np.exp(log_gamma * np.maximum(rel, 0.0)), 0.0)
    tail = np.broadcast_to(np.exp(log_gamma * (t_blk - 1.0 - idx[None, :, None])), decay.shape)
    inner = np.broadcast_to(np.exp(log_gamma * (idx[None, :, None] + 1.0)), decay.shape)
    return jnp.asarray(np.stack([decay, tail, inner], axis=1), dtype=F32)


def _silu(g):
    half_g = 0.5 * g
    return half_g + half_g * jnp.tanh(half_g)


def _mix_block(proj_ref, loga_ref, gno_ref, rgain_ref, rbias_ref, dec_ref, mix_ref, sg_ref, sr_ref):
    t_blk = MIX_BLOCK
    n_chunk = t_blk // GLA_CHUNK

    row = lax.broadcasted_iota(jnp.int32, (t_blk, t_blk), 0)
    col = lax.broadcasted_iota(jnp.int32, (t_blk, t_blk), 1)
    causal = col <= row
    chunk_shift = int(math.log2(GLA_CHUNK))
    gla_mask = causal & ((row >> chunk_shift) == (col >> chunk_shift))

    cum = gla_mask.astype(BF16)
    la_hi, la_mid, la_lo = _split3(loga_ref[...])
    b_all = (jnp.dot(cum, la_hi, preferred_element_type=F32)
             + jnp.dot(cum, la_mid, preferred_element_type=F32)
             + jnp.dot(cum, la_lo, preferred_element_type=F32))

    for h in range(GLA_HEADS):
        q = proj_ref[:, OFF_GQ + h * GLA_DK:OFF_GQ + (h + 1) * GLA_DK].astype(F32)
        k = proj_ref[:, OFF_GK + h * GLA_DK:OFF_GK + (h + 1) * GLA_DK].astype(F32)
        v = proj_ref[:, OFF_GV + h * GLA_DV:OFF_GV + (h + 1) * GLA_DV]
        g = proj_ref[:, OFF_GG + h * GLA_DV:OFF_GG + (h + 1) * GLA_DV].astype(F32)
        b = b_all[:, h * GLA_DK:(h + 1) * GLA_DK]
        b_last = [b[(c + 1) * GLA_CHUNK - 1:(c + 1) * GLA_CHUNK, :] for c in range(n_chunk)]
        pre = [jnp.zeros_like(b_last[0])]
        for c in range(1, n_chunk):
            pre.append(pre[-1] + b_last[c - 1])
        post = [jnp.zeros_like(b_last[0])]
        for c in range(n_chunk - 2, -1, -1):
            post.insert(0, post[0] + b_last[c + 1])
        by_rows = lambda vs: jnp.concatenate([jnp.broadcast_to(x, (GLA_CHUNK, GLA_DK)) for x in vs], axis=0)
        chunk = lambda a, c: a[c * GLA_CHUNK:(c + 1) * GLA_CHUNK]

        q_dec32 = q * (GLA_DK ** -0.5) * jnp.exp(b)
        q_dec = q_dec32.astype(BF16)
        q_blk = (q_dec32 * jnp.exp(by_rows(pre))).astype(BF16)
        k_dec = (k * jnp.exp(-b)).astype(BF16)
        k_tail32 = k * jnp.exp(by_rows(b_last) - b)
        k_tail = k_tail32.astype(BF16)
        k_end = (k_tail32 * jnp.exp(by_rows(post))).astype(BF16)

        s_rows = []
        for i in range(n_chunk):
            keys = []
            for j in range(n_chunk):
                if j + 1 == i:
                    keys.append(chunk(k_tail, j))
                elif j < i:
                    keys.append((chunk(k_tail32, j) * jnp.exp(pre[i] - pre[j + 1])).astype(BF16))
                else:
                    keys.append(chunk(k_dec, j))
            s_rows.append(lax.dot_general(chunk(q_dec, i), jnp.concatenate(keys, axis=0), _NT,
                                          preferred_element_type=F32))
        s = jnp.where(causal, jnp.concatenate(s_rows, axis=0), 0.0).astype(BF16)

        state = sg_ref[h]
        o = (jnp.dot(s, v, preferred_element_type=F32)
             + jnp.dot(q_blk, state.astype(BF16), preferred_element_type=F32))
        dcol = jnp.broadcast_to(jnp.exp(pre[-1] + b_last[-1]), (GLA_DK, GLA_DK)).T
        sg_ref[h] = (jnp.concatenate([dcol, dcol], axis=1) * state
                     + lax.dot_general(k_end, v, _TN, preferred_element_type=F32))

        y = o * lax.rsqrt(jnp.mean(o * o, axis=-1, keepdims=True) + EPS)
        y = y * gno_ref[:, h * GLA_DV:(h + 1) * GLA_DV]
        y = y * _silu(g)
        mix_ref[:, h * GLA_DV:(h + 1) * GLA_DV] = y.astype(mix_ref.dtype)

    for h in range(RET_HEADS):
        qb = proj_ref[:, OFF_RQ + h * RET_DK:OFF_RQ + (h + 1) * RET_DK]
        kb = proj_ref[:, OFF_RK + h * RET_DK:OFF_RK + (h + 1) * RET_DK]
        v = proj_ref[:, OFF_RV + h * RET_DV:OFF_RV + (h + 1) * RET_DV]
        g = proj_ref[:, OFF_RG + h * RET_DV:OFF_RG + (h + 1) * RET_DV].astype(F32)
        s = lax.dot_general(qb, kb, _NT, preferred_element_type=F32) * dec_ref[h, 0]
        o_intra = jnp.dot(s.astype(BF16), v, preferred_element_type=F32)
        k_tail = (kb.astype(F32) * dec_ref[h, 1]).astype(BF16)
        local = lax.dot_general(k_tail, v, _TN, preferred_element_type=F32)
        state = sr_ref[h]
        o_inter = jnp.dot(qb, state.astype(BF16), preferred_element_type=F32)
        o = o_intra + o_inter * dec_ref[h, 2]
        sr_ref[h] = _ret_gamma(h) ** t_blk * state + local

        mu = jnp.mean(o, axis=-1, keepdims=True)
        cen = o - mu
        y = cen * lax.rsqrt(jnp.mean(cen * cen, axis=-1, keepdims=True) + EPS)
        y = y * rgain_ref[:, h * RET_DV:(h + 1) * RET_DV] + rbias_ref[:, h * RET_DV:(h + 1) * RET_DV]
        y = y * _silu(g)
        mix_ref[:, GLA_V + h * RET_DV:GLA_V + (h + 1) * RET_DV] = y.astype(mix_ref.dtype)


def _mixer_outproj_kernel(proj_ref, loga_ref, gno_ref, rgain_ref, rbias_ref, dec_ref, x_ref, wout_ref,
                          wup_ref, wdown_ref, *rest, blocks_per_seq, cast_next_win,
                          win_full_slabs=0, win_tail_rows=0):
    if cast_next_win:
        win_ref, out_ref, wup_b_ref, wdown_b_ref, win_b_ref, sg_ref, sr_ref, mix_a, mix_b = rest
    else:
        out_ref, wup_b_ref, wdown_b_ref, sg_ref, sr_ref, mix_a, mix_b = rest
    t = pl.program_id(0)

    @pl.when(lax.rem(t, blocks_per_seq) == 0)
    def _():
        sg_ref[...] = jnp.zeros_like(sg_ref)
        sr_ref[...] = jnp.zeros_like(sr_ref)

    @pl.when(t == 0)
    def _():
        mix_b[...] = jnp.zeros_like(mix_b)

    if cast_next_win:
        _win_cast_step(t, win_ref, win_b_ref, win_full_slabs, win_tail_rows)

    def step(src, dst):
        for k in range(wup_b_ref.shape[0]):
            cols = slice(k * MLP_FF_TILE, (k + 1) * MLP_FF_TILE)
            wup_b_ref[k] = wup_ref[:, cols].astype(wup_b_ref.dtype)
        wdown_b_ref[...] = wdown_ref[...].astype(wdown_b_ref.dtype)
        for blk in range(MIX_STEP_BLOCKS):
            rows = pl.ds(blk * MIX_BLOCK, MIX_BLOCK)
            _mix_block(proj_ref.at[rows], loga_ref.at[rows], gno_ref, rgain_ref, rbias_ref, dec_ref,
                       dst.at[rows], sg_ref, sr_ref)
        out_ref[...] = x_ref[...] + jnp.dot(src[...], wout_ref[...], preferred_element_type=F32)

    n_blk = pl.num_programs(0) - 1
    parity = lax.rem(t, 2)

    @pl.when((parity == 0) & (t < n_blk))
    def _():
        step(mix_b, mix_a)

    @pl.when(parity == 1)
    def _():
        step(mix_a, mix_b)

    @pl.when(t == n_blk)
    def _():
        out_ref[...] = x_ref[...] + jnp.dot(mix_b[...], wout_ref[...], preferred_element_type=F32)


def _mixers_outproj(proj, loga, gla_norm, ret_gain, ret_bias, x, w_out, w_up, w_down, next_win_t, layer, seq):
    n_tok = proj.shape[0]
    t_blk = MIX_BLOCK * MIX_STEP_BLOCKS
    n_blk = n_tok // t_blk
    assert n_blk % 2 == 0 and seq % t_blk == 0
    n_ff = D_FF // MLP_FF_TILE
    cur = lambda t: (jnp.minimum(t, n_blk - 1), 0)
    prev = lambda t: (jnp.maximum(t - 1, 0), 0)
    const = lambda t: (layer, 0, 0)
    slab = lambda t: (layer, jnp.minimum(t, n_blk - 1), 0)
    up_rows, down_rows = D_MODEL // n_blk, D_FF // n_blk
    cast_next_win = next_win_t is not None
    static = dict(blocks_per_seq=seq // t_blk, cast_next_win=cast_next_win)

    in_specs = [pl.BlockSpec((t_blk, D_PROJ), cur),
                pl.BlockSpec((t_blk, GLA_QK), cur),
                pl.BlockSpec((None, 1, GLA_V), const),
                pl.BlockSpec((None, 1, RET_V), const),
                pl.BlockSpec((None, 1, RET_V), const),
                pl.BlockSpec((RET_HEADS, 3, MIX_BLOCK, MIX_BLOCK), lambda t: (0, 0, 0, 0)),
                pl.BlockSpec((t_blk, D_MODEL), prev),
                pl.BlockSpec((D_MODEL, D_MODEL), lambda t: (0, 0)),
                pl.BlockSpec((None, up_rows, D_FF), slab),
                pl.BlockSpec((None, down_rows, D_MODEL), slab)]
    out_specs = [pl.BlockSpec((t_blk, D_MODEL), prev),
                 pl.BlockSpec((n_ff, up_rows, MLP_FF_TILE), lambda t: (0, jnp.minimum(t, n_blk - 1), 0)),
                 pl.BlockSpec((down_rows, D_MODEL), cur)]
    out_shape = [jax.ShapeDtypeStruct((n_tok, D_MODEL), F32),
                 jax.ShapeDtypeStruct((n_ff, D_MODEL, MLP_FF_TILE), BF16),
                 jax.ShapeDtypeStruct((D_FF, D_MODEL), BF16)]
    operands = [proj, loga, gla_norm, ret_gain, ret_bias, _ret_decay_tables(MIX_BLOCK), x, w_out, w_up, w_down]
    if cast_next_win:
        w_in_t, next_layer = next_win_t
        win_in, win_out, win_shape, full_slabs, tail_rows = _win_cast_specs(w_in_t, next_layer, WIN_SLAB, n_blk)
        in_specs.append(win_in)
        out_specs.append(win_out)
        out_shape.append(win_shape)
        static.update(win_full_slabs=full_slabs, win_tail_rows=tail_rows)
        operands.append(w_in_t)

    return pl.pallas_call(
        functools.partial(_mixer_outproj_kernel, **static),
        grid=(n_blk + 1,),
        in_specs=in_specs,
        out_specs=out_specs,
        out_shape=out_shape,
        scratch_shapes=[pltpu.VMEM((GLA_HEADS, GLA_DK, GLA_DV), F32),
                        pltpu.VMEM((RET_HEADS, RET_DK, RET_DV), F32),
                        pltpu.VMEM((t_blk, D_MODEL), BF16),
                        pltpu.VMEM((t_blk, D_MODEL), BF16)],
        compiler_params=_params(1),
        name="mixers_outproj",
    )(*operands)


def _mlp_kernel(x_ref, g_ref, wu_ref, wd_ref, fg_ref, o_ref, h_ref, *, apply_final_norm):
    k = pl.program_id(1)

    def ff_tile(h):
        u = jnp.dot(h, wu_ref[...], preferred_element_type=F32)
        a = jnp.square(jnp.maximum(u, 0.0)).astype(BF16)
        return jnp.dot(a, wd_ref[...], preferred_element_type=F32)

    @pl.when(k == 0)
    def _():
        x = x_ref[...]
        h = _rms(x, g_ref[...]).astype(BF16)
        h_ref[...] = h
        o_ref[...] = x + ff_tile(h)

    @pl.when(k != 0)
    def _():
        o_ref[...] += ff_tile(h_ref[...])

    if apply_final_norm:
        @pl.when(k == pl.num_programs(1) - 1)
        def _():
            o_ref[...] = _rms(o_ref[...], fg_ref[...])


def _mlp(x, gain, w_up, w_down, final_gain, layer, apply_final_norm, tm=1024):
    n_tok = x.shape[0]
    tf = MLP_FF_TILE
    return pl.pallas_call(
        functools.partial(_mlp_kernel, apply_final_norm=apply_final_norm),
        grid=(n_tok // tm, D_FF // tf),
        in_specs=[pl.BlockSpec((tm, D_MODEL), lambda i, k: (i, 0)),
                  pl.BlockSpec((None, 1, D_MODEL), lambda i, k: (layer, 0, 0)),
                  pl.BlockSpec((None, D_MODEL, tf), lambda i, k: (k, 0, 0)),
                  pl.BlockSpec((tf, D_MODEL), lambda i, k: (k, 0)),
                  pl.BlockSpec((1, D_MODEL), lambda i, k: (0, 0))],
        out_specs=pl.BlockSpec((tm, D_MODEL), lambda i, k: (i, 0)),
        out_shape=jax.ShapeDtypeStruct((n_tok, D_MODEL), F32),
        scratch_shapes=[pltpu.VMEM((tm, D_MODEL), BF16)],
        compiler_params=_params(2),
        name="norm_mlp_residual",
    )(x, gain, w_up, w_down, final_gain)


def kernel(x, positions, attn_norm, w_in, gla_gate_up, gla_gate_bias, gla_out_norm,
           ret_norm_gain, ret_norm_bias, w_out, mlp_norm, w_up, w_down, final_norm):
    batch, seq, d_model = x.shape
    depth = w_in.shape[0]
    assert d_model == D_MODEL and seq % (MIX_BLOCK * MIX_STEP_BLOCKS) == 0
    n_tok = batch * seq
    xf = x.reshape(n_tok, d_model)

    pad = LANES - GLA_GATE_RANK
    w_in_t = jnp.swapaxes(w_in, 1, 2)
    cos, sin, w_t = _rope_tables(positions, w_in_t)
    w_gr_t = jnp.pad(w_in[:, :, GATE_LO:GATE_HI].swapaxes(1, 2), ((0, 0), (0, pad), (0, 0))).astype(BF16)
    gate_up = jnp.pad(gla_gate_up, ((0, 0), (0, pad), (0, 0))).astype(BF16)
    row3 = lambda a: a.reshape(depth, 1, -1)
    attn_g, gate_b, mlp_g = row3(attn_norm), row3(gla_gate_bias), row3(mlp_norm)
    gla_g, ret_g, ret_b = row3(gla_out_norm), row3(ret_norm_gain), row3(ret_norm_bias)
    final_g = final_norm.reshape(1, -1)

    for l in range(depth):
        last = l == depth - 1
        proj, loga, w_out_b = _inproj(xf, attn_g, w_t, w_gr_t, gate_up, gate_b, cos, sin, w_out, l)
        outs = _mixers_outproj(proj, loga, gla_g, ret_g, ret_b, xf, w_out_b, w_up, w_down,
                               None if last else (w_in_t, l + 1), l, seq)
        xf, w_up_b, w_down_b = outs[:3]
        if not last:
            w_t = outs[3]
        xf = _mlp(xf, mlp_g, w_up_b, w_down_b, final_g, l, apply_final_norm=last)
    return xf.reshape(batch, seq, d_model)
```

```python
import functools
import math

import jax
import jax.numpy as jnp
import numpy as np
from jax import lax
from jax.experimental import pallas as pl
from jax.experimental.pallas import tpu as pltpu

D_MODEL = 2048
GLA_HEADS = 4
GLA_DK = 128
GLA_DV = 256
GLA_GATE_RANK = 16
GLA_TAU = 16.0
GLA_CHUNK = 64
RET_HEADS = 4
RET_DK = 256
RET_DV = 256
ROPE_BASE = 10000.0
D_FF = 4 * D_MODEL
EPS = 1e-6

GLA_QK = GLA_HEADS * GLA_DK
GLA_V = GLA_HEADS * GLA_DV
RET_QK = RET_HEADS * RET_DK
RET_V = RET_HEADS * RET_DV
D_PROJ = 2 * GLA_QK + 2 * GLA_V + 2 * RET_QK + 2 * RET_V

OFF_GQ = 0
OFF_GK = OFF_GQ + GLA_QK
OFF_GV = OFF_GK + GLA_QK
OFF_GG = OFF_GV + GLA_V
OFF_RQ = OFF_GG + GLA_V
OFF_RK = OFF_RQ + RET_QK
OFF_RV = OFF_RK + RET_QK
OFF_RG = OFF_RV + RET_V

LANES = 128
MIX_BLOCK = 256
MIX_STEP_BLOCKS = 1
MLP_FF_TILE = 1024
WIN_SLAB = 128
ROPE_WIN_SLAB = 512
VMEM_LIMIT = 60 * 1024 * 1024

F32 = jnp.float32
BF16 = jnp.bfloat16

_NT = (((1,), (1,)), ((), ()))
_TN = (((0,), (0,)), ((), ()))


def _params(n_axes):
    return pltpu.CompilerParams(
        dimension_semantics=("arbitrary",) * n_axes, vmem_limit_bytes=VMEM_LIMIT)


def _rms(x, gain):
    return x * lax.rsqrt(jnp.mean(x * x, axis=-1, keepdims=True) + EPS) * gain


def _win_cast_specs(w_in_t, layer, slab, n_steps):
    d_in = w_in_t.shape[1]
    full_slabs, tail_rows = divmod(d_in, slab)
    n_slabs = full_slabs + (tail_rows > 0)
    last = d_in - slab
    assert last % GLA_GATE_RANK == 0 and tail_rows % GLA_GATE_RANK == 0 and n_slabs <= n_steps
    row = lambda t: pl.multiple_of(jnp.minimum(t * slab, last), GLA_GATE_RANK)
    in_spec = pl.BlockSpec((pl.Element(1), pl.Element(slab), pl.Element(D_MODEL)), lambda t: (layer, row(t), 0))
    out_spec = pl.BlockSpec((slab, D_MODEL), lambda t: (jnp.minimum(t, n_slabs - 1), 0))
    out_shape = jax.ShapeDtypeStruct((n_slabs * slab, D_MODEL), BF16)
    return in_spec, out_spec, out_shape, full_slabs, tail_rows


def _win_cast_step(t, win_ref, win_b_ref, full_slabs, tail_rows):
    slab = win_b_ref.shape[0]

    @pl.when(t < full_slabs)
    def _():
        win_b_ref[...] = win_ref[0].astype(win_b_ref.dtype)

    @pl.when(t >= full_slabs)
    def _():
        win_b_ref[...] = jnp.zeros_like(win_b_ref)
        win_b_ref[:tail_rows, :] = win_ref[0, slab - tail_rows:, :].astype(win_b_ref.dtype)


def _rope_kernel(pos_ref, invf_ref, win_ref, cos_ref, sin_ref, win_b_ref, *, win_full_slabs, win_tail_rows):
    _win_cast_step(pl.program_id(0), win_ref, win_b_ref, win_full_slabs, win_tail_rows)
    rows = pos_ref.shape[0]
    invf = invf_ref[...]
    for r in range(rows):
        p = pos_ref[r:r + 1, :].astype(F32)
        col = jnp.broadcast_to(p, (LANES, LANES)).T
        ang = col * invf
        cos_ref[r * LANES:(r + 1) * LANES, :] = jnp.cos(ang)
        sin_ref[r * LANES:(r + 1) * LANES, :] = jnp.sin(ang)


def _rope_tables(positions, w_in_t):
    n_tok = positions.size
    rows = n_tok // LANES
    rb = 8
    pos2d = positions.reshape(rows, LANES)
    inv_freq = (ROPE_BASE ** (-jnp.arange(0, RET_DK, 2, dtype=F32) / RET_DK)).reshape(1, RET_DK // 2)
    out = jax.ShapeDtypeStruct((n_tok, RET_DK // 2), F32)
    n_steps = rows // rb
    win_in, win_out, win_shape, full_slabs, tail_rows = _win_cast_specs(w_in_t, 0, ROPE_WIN_SLAB, n_steps)
    return pl.pallas_call(
        functools.partial(_rope_kernel, win_full_slabs=full_slabs, win_tail_rows=tail_rows),
        grid=(n_steps,),
        in_specs=[pl.BlockSpec((rb, LANES), lambda i: (i, 0)),
                  pl.BlockSpec((1, RET_DK // 2), lambda i: (0, 0)),
                  win_in],
        out_specs=[pl.BlockSpec((rb * LANES, RET_DK // 2), lambda i: (i, 0))] * 2 + [win_out],
        out_shape=[out, out, win_shape],
        compiler_params=_params(1),
        name="rope_tables",
    )(pos2d, inv_freq, w_in_t)


GATE_LO = 2 * GLA_QK + 2 * GLA_V
GATE_HI = GATE_LO + GLA_GATE_RANK


def _inproj_kernel(x_ref, g_ref, w_ref, wgr_ref, gup_ref, gb_ref, cos_ref, sin_ref, wout_ref,
                   proj_ref, loga_ref, wout_b_ref, h_ref, *, q_tile, k_tile):
    j = pl.program_id(1)

    def project(h):
        return lax.dot_general(h, w_ref[...], _NT, preferred_element_type=F32)

    @pl.when(j == 0)
    def _():
        wout_b_ref[...] = wout_ref[...].astype(wout_b_ref.dtype)
        hb = _rms(x_ref[...], g_ref[...]).astype(BF16)
        h_ref[...] = hb
        gr = lax.dot_general(hb, wgr_ref[...], _NT, preferred_element_type=F32)
        z = jnp.dot(gr.astype(BF16), gup_ref[...], preferred_element_type=F32) + gb_ref[...]
        log_sig = jnp.minimum(z, 0.0) - jnp.log(1.0 + jnp.exp(-jnp.abs(z)))
        loga_ref[...] = log_sig * (1.0 / GLA_TAU)
        proj_ref[...] = project(hb).astype(proj_ref.dtype)

    is_rotary = (j == q_tile) | (j == k_tile)

    @pl.when((j != 0) & jnp.logical_not(is_rotary))
    def _():
        proj_ref[...] = project(h_ref[...]).astype(proj_ref.dtype)

    @pl.when(is_rotary)
    def _():
        p = project(h_ref[...])
        scale = jnp.where(j == k_tile, RET_DK ** -0.5, 1.0).astype(F32)
        cos = cos_ref[...] * scale
        sin = sin_ref[...] * scale
        half = RET_DK // 2
        for h in range(RET_HEADS):
            lo = slice(h * RET_DK, h * RET_DK + half)
            hi = slice(h * RET_DK + half, (h + 1) * RET_DK)
            t1, t2 = p[:, lo], p[:, hi]
            proj_ref[:, lo] = (t1 * cos - t2 * sin).astype(proj_ref.dtype)
            proj_ref[:, hi] = (t2 * cos + t1 * sin).astype(proj_ref.dtype)


def _inproj(x, gain, w_t, w_gr_t, gate_up, gate_bias, cos, sin, w_out, layer, tm=1024):
    n_tok = x.shape[0]
    wout_rows = D_MODEL // (n_tok // tm)
    tn = RET_QK
    assert OFF_RQ % tn == 0 and OFF_RK % tn == 0 and GATE_LO % tn == 0 and OFF_RQ >= tn

    def w_rows(i, j):
        start = j * tn
        row = jnp.where(start < GATE_LO, start, start + GLA_GATE_RANK)
        return pl.multiple_of(row, GLA_GATE_RANK), 0

    return pl.pallas_call(
        functools.partial(_inproj_kernel, q_tile=OFF_RQ // tn, k_tile=OFF_RK // tn),
        grid=(n_tok // tm, D_PROJ // tn),
        in_specs=[pl.BlockSpec((tm, D_MODEL), lambda i, j: (i, 0)),
                  pl.BlockSpec((None, 1, D_MODEL), lambda i, j: (layer, 0, 0)),
                  pl.BlockSpec((pl.Element(tn), pl.Element(D_MODEL)), w_rows),
                  pl.BlockSpec((None, LANES, D_MODEL), lambda i, j: (layer, 0, 0)),
                  pl.BlockSpec((None, LANES, GLA_QK), lambda i, j: (layer, 0, 0)),
                  pl.BlockSpec((None, 1, GLA_QK), lambda i, j: (layer, 0, 0)),
                  pl.BlockSpec((tm, RET_DK // 2), lambda i, j: (i, 0)),
                  pl.BlockSpec((tm, RET_DK // 2), lambda i, j: (i, 0)),
                  pl.BlockSpec((None, wout_rows, D_MODEL), lambda i, j: (layer, i, 0))],
        out_specs=[pl.BlockSpec((tm, tn), lambda i, j: (i, j)),
                   pl.BlockSpec((tm, GLA_QK), lambda i, j: (i, 0)),
                   pl.BlockSpec((wout_rows, D_MODEL), lambda i, j: (i, 0))],
        out_shape=[jax.ShapeDtypeStruct((n_tok, D_PROJ), BF16),
                   jax.ShapeDtypeStruct((n_tok, GLA_QK), F32),
                   jax.ShapeDtypeStruct((D_MODEL, D_MODEL), BF16)],
        scratch_shapes=[pltpu.VMEM((tm, D_MODEL), BF16)],
        compiler_params=_params(2),
        name="norm_inproj",
    )(x, gain, w_t, w_gr_t, gate_up, gate_bias, cos, sin, w_out)


def _split2(x):
    hi = x.astype(BF16)
    lo = (x - hi.astype(F32)).astype(BF16)
    return hi, lo


def _ret_gamma(h):
    return 1.0 - 2.0 ** (-5.0 - h)


def _ret_decay_tables(t_blk):
    assert t_blk == RET_DK == RET_DV
    log_gamma = np.log1p(-(2.0 ** (-5.0 - np.arange(RET_HEADS))))[:, None, None]
    idx = np.arange(t_blk, dtype=np.float64)
    rel = idx[:, None] - idx[None, :]
    decay = np.where(rel >= 0.0, np.exp(log_gamma * np.maximum(rel, 0.0)), 0.0)
    tail = np.broadcast_to(np.exp(log_gamma * (t_blk - 1.0 - idx[None, :, None])), decay.shape)
    inner = np.broadcast_to(np.exp(log_gamma * (idx[None, :, None] + 1.0)), decay.shape)
    return jnp.asarray(np.stack([decay, tail, inner], axis=1), dtype=F32)


def _silu(g):
    half_g = 0.5 * g
    return half_g + half_g * jnp.tanh(half_g)


def _mix_block(proj_ref, loga_ref, gno_ref, rgain_ref, rbias_ref, dec_ref, mix_ref, sg_ref, sr_ref):
    t_blk = MIX_BLOCK
    n_chunk = t_blk // GLA_CHUNK

    row = lax.broadcasted_iota(jnp.int32, (t_blk, t_blk), 0)
    col = lax.broadcasted_iota(jnp.int32, (t_blk, t_blk), 1)
    causal = col <= row
    chunk_shift = int(math.log2(GLA_CHUNK))
    gla_mask = causal & ((row >> chunk_shift) == (col >> chunk_shift))

    cum = gla_mask.astype(BF16)
    la_hi, la_lo = _split2(loga_ref[...])
    b_all = (jnp.dot(cum, la_hi, preferred_element_type=F32)
             + jnp.dot(cum, la_lo, preferred_element_type=F32))

    for h in range(GLA_HEADS):
        q = proj_ref[:, OFF_GQ + h * GLA_DK:OFF_GQ + (h + 1) * GLA_DK].astype(F32)
        k = proj_ref[:, OFF_GK + h * GLA_DK:OFF_GK + (h + 1) * GLA_DK].astype(F32)
        v = proj_ref[:, OFF_GV + h * GLA_DV:OFF_GV + (h + 1) * GLA_DV]
        g = proj_ref[:, OFF_GG + h * GLA_DV:OFF_GG + (h + 1) * GLA_DV].astype(F32)
        b = b_all[:, h * GLA_DK:(h + 1) * GLA_DK]
        b_last = [b[(c + 1) * GLA_CHUNK - 1:(c + 1) * GLA_CHUNK, :] for c in range(n_chunk)]
        pre = [jnp.zeros_like(b_last[0])]
        for c in range(1, n_chunk):
            pre.append(pre[-1] + b_last[c - 1])
        post = [jnp.zeros_like(b_last[0])]
        for c in range(n_chunk - 2, -1, -1):
            post.insert(0, post[0] + b_last[c + 1])
        by_rows = lambda vs: jnp.concatenate([jnp.broadcast_to(x, (GLA_CHUNK, GLA_DK)) for x in vs], axis=0)
        chunk = lambda a, c: a[c * GLA_CHUNK:(c + 1) * GLA_CHUNK]

        q_dec32 = q * (GLA_DK ** -0.5) * jnp.exp(b)
        q_dec = q_dec32.astype(BF16)
        q_blk = (q_dec32 * jnp.exp(by_rows(pre))).astype(BF16)
        k_dec = (k * jnp.exp(-b)).astype(BF16)
        k_tail32 = k * jnp.exp(by_rows(b_last) - b)
        k_tail = k_tail32.astype(BF16)
        k_end = (k_tail32 * jnp.exp(by_rows(post))).astype(BF16)

        s_rows = []
        for i in range(n_chunk):
            keys = []
            for j in range(n_chunk):
                if j + 1 == i:
                    keys.append(chunk(k_tail, j))
                elif j < i:
                    keys.append((chunk(k_tail32, j) * jnp.exp(pre[i] - pre[j + 1])).astype(BF16))
                else:
                    keys.append(chunk(k_dec, j))
            s_rows.append(lax.dot_general(chunk(q_dec, i), jnp.concatenate(keys, axis=0), _NT,
                                          preferred_element_type=F32))
        s = jnp.where(causal, jnp.concatenate(s_rows, axis=0), 0.0).astype(BF16)

        state = sg_ref[h]
        o = (jnp.dot(s, v, preferred_element_type=F32)
             + jnp.dot(q_blk, state.astype(BF16), preferred_element_type=F32))
        dcol = jnp.broadcast_to(jnp.exp(pre[-1] + b_last[-1]), (GLA_DK, GLA_DK)).T
        sg_ref[h] = (jnp.concatenate([dcol, dcol], axis=1) * state
                     + lax.dot_general(k_end, v, _TN, preferred_element_type=F32))

        y = o * lax.rsqrt(jnp.mean(o * o, axis=-1, keepdims=True) + EPS)
        y = y * gno_ref[:, h * GLA_DV:(h + 1) * GLA_DV]
        y = y * _silu(g)
        mix_ref[:, h * GLA_DV:(h + 1) * GLA_DV] = y.astype(mix_ref.dtype)

    for h in range(RET_HEADS):
        qb = proj_ref[:, OFF_RQ + h * RET_DK:OFF_RQ + (h + 1) * RET_DK]
        kb = proj_ref[:, OFF_RK + h * RET_DK:OFF_RK + (h + 1) * RET_DK]
        v = proj_ref[:, OFF_RV + h * RET_DV:OFF_RV + (h + 1) * RET_DV]
        g = proj_ref[:, OFF_RG + h * RET_DV:OFF_RG + (h + 1) * RET_DV].astype(F32)
        s = lax.dot_general(qb, kb, _NT, preferred_element_type=F32) * dec_ref[h, 0]
        o_intra = jnp.dot(s.astype(BF16), v, preferred_element_type=F32)
        k_tail = (kb.astype(F32) * dec_ref[h, 1]).astype(BF16)
        local = lax.dot_general(k_tail, v, _TN, preferred_element_type=F32)
        state = sr_ref[h]
        o_inter = jnp.dot(qb, state.astype(BF16), preferred_element_type=F32)
        o = o_intra + o_inter * dec_ref[h, 2]
        sr_ref[h] = _ret_gamma(h) ** t_blk * state + local

        mu = jnp.mean(o, axis=-1, keepdims=True)
        cen = o - mu
        y = cen * lax.rsqrt(jnp.mean(cen * cen, axis=-1, keepdims=True) + EPS)
        y = y * rgain_ref[:, h * RET_DV:(h + 1) * RET_DV] + rbias_ref[:, h * RET_DV:(h + 1) * RET_DV]
        y = y * _silu(g)
        mix_ref[:, GLA_V + h * RET_DV:GLA_V + (h + 1) * RET_DV] = y.astype(mix_ref.dtype)


def _mixer_outproj_kernel(proj_ref, loga_ref, gno_ref, rgain_ref, rbias_ref, dec_ref, x_ref, wout_ref,
                          wup_ref, wdown_ref, *rest, blocks_per_seq, cast_next_win,
                          win_full_slabs=0, win_tail_rows=0):
    if cast_next_win:
        win_ref, out_ref, wup_b_ref, wdown_b_ref, win_b_ref, sg_ref, sr_ref, mix_a, mix_b = rest
    else:
        out_ref, wup_b_ref, wdown_b_ref, sg_ref, sr_ref, mix_a, mix_b = rest
    t = pl.program_id(0)

    @pl.when(lax.rem(t, blocks_per_seq) == 0)
    def _():
        sg_ref[...] = jnp.zeros_like(sg_ref)
        sr_ref[...] = jnp.zeros_like(sr_ref)

    @pl.when(t == 0)
    def _():
        mix_b[...] = jnp.zeros_like(mix_b)

    if cast_next_win:
        _win_cast_step(t, win_ref, win_b_ref, win_full_slabs, win_tail_rows)

    def step(src, dst):
        for k in range(wup_b_ref.shape[0]):
            cols = slice(k * MLP_FF_TILE, (k + 1) * MLP_FF_TILE)
            wup_b_ref[k] = wup_ref[:, cols].astype(wup_b_ref.dtype)
        wdown_b_ref[...] = wdown_ref[...].astype(wdown_b_ref.dtype)
        for blk in range(MIX_STEP_BLOCKS):
            rows = pl.ds(blk * MIX_BLOCK, MIX_BLOCK)
            _mix_block(proj_ref.at[rows], loga_ref.at[rows], gno_ref, rgain_ref, rbias_ref, dec_ref,
                       dst.at[rows], sg_ref, sr_ref)
        out_ref[...] = x_ref[...] + jnp.dot(src[...], wout_ref[...], preferred_element_type=F32)

    n_blk = pl.num_programs(0) - 1
    parity = lax.rem(t, 2)

    @pl.when((parity == 0) & (t < n_blk))
    def _():
        step(mix_b, mix_a)

    @pl.when(parity == 1)
    def _():
        step(mix_a, mix_b)

    @pl.when(t == n_blk)
    def _():
        out_ref[...] = x_ref[...] + jnp.dot(mix_b[...], wout_ref[...], preferred_element_type=F32)


def _mixers_outproj(proj, loga, gla_norm, ret_gain, ret_bias, x, w_out, w_up, w_down, next_win_t, layer, seq):
    n_tok = proj.shape[0]
    t_blk = MIX_BLOCK * MIX_STEP_BLOCKS
    n_blk = n_tok // t_blk
    assert n_blk % 2 == 0 and seq % t_blk == 0
    n_ff = D_FF // MLP_FF_TILE
    cur = lambda t: (jnp.minimum(t, n_blk - 1), 0)
    prev = lambda t: (jnp.maximum(t - 1, 0), 0)
    const = lambda t: (layer, 0, 0)
    slab = lambda t: (layer, jnp.minimum(t, n_blk - 1), 0)
    up_rows, down_rows = D_MODEL // n_blk, D_FF // n_blk
    cast_next_win = next_win_t is not None
    static = dict(blocks_per_seq=seq // t_blk, cast_next_win=cast_next_win)

    in_specs = [pl.BlockSpec((t_blk, D_PROJ), cur),
                pl.BlockSpec((t_blk, GLA_QK), cur),
                pl.BlockSpec((None, 1, GLA_V), const),
                pl.BlockSpec((None, 1, RET_V), const),
                pl.BlockSpec((None, 1, RET_V), const),
                pl.BlockSpec((RET_HEADS, 3, MIX_BLOCK, MIX_BLOCK), lambda t: (0, 0, 0, 0)),
                pl.BlockSpec((t_blk, D_MODEL), prev),
                pl.BlockSpec((D_MODEL, D_MODEL), lambda t: (0, 0)),
                pl.BlockSpec((None, up_rows, D_FF), slab),
                pl.BlockSpec((None, down_rows, D_MODEL), slab)]
    out_specs = [pl.BlockSpec((t_blk, D_MODEL), prev),
                 pl.BlockSpec((n_ff, up_rows, MLP_FF_TILE), lambda t: (0, jnp.minimum(t, n_blk - 1), 0)),
                 pl.BlockSpec((down_rows, D_MODEL), cur)]
    out_shape = [jax.ShapeDtypeStruct((n_tok, D_MODEL), F32),
                 jax.ShapeDtypeStruct((n_ff, D_MODEL, MLP_FF_TILE), BF16),
                 jax.ShapeDtypeStruct((D_FF, D_MODEL), BF16)]
    operands = [proj, loga, gla_norm, ret_gain, ret_bias, _ret_decay_tables(MIX_BLOCK), x, w_out, w_up, w_down]
    if cast_next_win:
        w_in_t, next_layer = next_win_t
        win_in, win_out, win_shape, full_slabs, tail_rows = _win_cast_specs(w_in_t, next_layer, WIN_SLAB, n_blk)
        in_specs.append(win_in)
        out_specs.append(win_out)
        out_shape.append(win_shape)
        static.update(win_full_slabs=full_slabs, win_tail_rows=tail_rows)
        operands.append(w_in_t)

    return pl.pallas_call(
        functools.partial(_mixer_outproj_kernel, **static),
        grid=(n_blk + 1,),
        in_specs=in_specs,
        out_specs=out_specs,
        out_shape=out_shape,
        scratch_shapes=[pltpu.VMEM((GLA_HEADS, GLA_DK, GLA_DV), F32),
                        pltpu.VMEM((RET_HEADS, RET_DK, RET_DV), F32),
                        pltpu.VMEM((t_blk, D_MODEL), BF16),
                        pltpu.VMEM((t_blk, D_MODEL), BF16)],
        compiler_params=_params(1),
        name="mixers_outproj",
    )(*operands)


def _mlp_kernel(x_ref, g_ref, wu_ref, wd_ref, fg_ref, o_ref, h_ref, *, apply_final_norm):
    k = pl.program_id(1)

    def ff_tile(h):
        u = jnp.dot(h, wu_ref[...], preferred_element_type=F32)
        a = jnp.square(jnp.maximum(u, 0.0)).astype(BF16)
        return jnp.dot(a, wd_ref[...], preferred_element_type=F32)

    @pl.when(k == 0)
    def _():
        x = x_ref[...]
        h = _rms(x, g_ref[...]).astype(BF16)
        h_ref[...] = h
        o_ref[...] = x + ff_tile(h)

    @pl.when(k != 0)
    def _():
        o_ref[...] += ff_tile(h_ref[...])

    if apply_final_norm:
        @pl.when(k == pl.num_programs(1) - 1)
        def _():
            o_ref[...] = _rms(o_ref[...], fg_ref[...])


def _mlp(x, gain, w_up, w_down, final_gain, layer, apply_final_norm, tm=1024):
    n_tok = x.shape[0]
    tf = MLP_FF_TILE
    return pl.pallas_call(
        functools.partial(_mlp_kernel, apply_final_norm=apply_final_norm),
        grid=(n_tok // tm, D_FF // tf),
        in_specs=[pl.BlockSpec((tm, D_MODEL), lambda i, k: (i, 0)),
                  pl.BlockSpec((None, 1, D_MODEL), lambda i, k: (layer, 0, 0)),
                  pl.BlockSpec((None, D_MODEL, tf), lambda i, k: (k, 0, 0)),
                  pl.BlockSpec((tf, D_MODEL), lambda i, k: (k, 0)),
                  pl.BlockSpec((1, D_MODEL), lambda i, k: (0, 0))],
        out_specs=pl.BlockSpec((tm, D_MODEL), lambda i, k: (i, 0)),
        out_shape=jax.ShapeDtypeStruct((n_tok, D_MODEL), F32),
        scratch_shapes=[pltpu.VMEM((tm, D_MODEL), BF16)],
        compiler_params=_params(2),
        name="norm_mlp_residual",
    )(x, gain, w_up, w_down, final_gain)


def kernel(x, positions, attn_norm, w_in, gla_gate_up, gla_gate_bias, gla_out_norm,
           ret_norm_gain, ret_norm_bias, w_out, mlp_norm, w_up, w_down, final_norm):
    batch, seq, d_model = x.shape
    depth = w_in.shape[0]
    assert d_model == D_MODEL and seq % (MIX_BLOCK * MIX_STEP_BLOCKS) == 0
    n_tok = batch * seq
    xf = x.reshape(n_tok, d_model)

    pad = LANES - GLA_GATE_RANK
    w_in_t = jnp.swapaxes(w_in, 1, 2)
    cos, sin, w_t = _rope_tables(positions, w_in_t)
    w_gr_t = jnp.pad(w_in[:, :, GATE_LO:GATE_HI].swapaxes(1, 2), ((0, 0), (0, pad), (0, 0))).astype(BF16)
    gate_up = jnp.pad(gla_gate_up, ((0, 0), (0, pad), (0, 0))).astype(BF16)
    row3 = lambda a: a.reshape(depth, 1, -1)
    attn_g, gate_b, mlp_g = row3(attn_norm), row3(gla_gate_bias), row3(mlp_norm)
    gla_g, ret_g, ret_b = row3(gla_out_norm), row3(ret_norm_gain), row3(ret_norm_bias)
    final_g = final_norm.reshape(1, -1)

    for l in range(depth):
        last = l == depth - 1
        proj, loga, w_out_b = _inproj(xf, attn_g, w_t, w_gr_t, gate_up, gate_b, cos, sin, w_out, l)
        outs = _mixers_outproj(proj, loga, gla_g, ret_g, ret_b, xf, w_out_b, w_up, w_down,
                               None if last else (w_in_t, l + 1), l, seq)
        xf, w_up_b, w_down_b = outs[:3]
        if not last:
            w_t = outs[3]
        xf = _mlp(xf, mlp_g, w_up_b, w_down_b, final_g, l, apply_final_norm=last)
    return xf.reshape(batch, seq, d_model)
```
